```python
import jax, jax.numpy as jnp
from jax import lax
import numpy as np

D_MODEL = 1024
BATCH = 8
SEQ = 8192
DEPTH = 1
DEC_BATCH = 2
DEC_SEQ = 16384
PAST_LEN = 128

N_MEM = 256
EPS = 1e-6
ROPE_THETA = 10000.0
Q_BLOCK = 128
MLA_HEADS = 8
MLA_NOPE = 64
MLA_ROPE = 32
MLA_V = 64
MLA_QK = MLA_NOPE + MLA_ROPE
Q_LORA = 384
KV_LORA = 256
SWA_HEADS = 8
SWA_KV_HEADS = 2
SWA_HD = 64
WINDOW = 128
BLOCK = 128
MEM_HEADS = 4
MEM_HD = 128
N_BRANCH = 3
D_FF = 2816
CONV_W = 3
SPLITS = (Q_LORA, KV_LORA, MLA_ROPE, SWA_HEADS * SWA_HD, SWA_KV_HEADS * SWA_HD, SWA_KV_HEADS * SWA_HD, MEM_HEADS * MEM_HD, N_BRANCH * D_MODEL)
D_IN = Q_LORA + KV_LORA + MLA_ROPE + SWA_HEADS * SWA_HD + 2 * SWA_KV_HEADS * SWA_HD + MEM_HEADS * MEM_HD + N_BRANCH * D_MODEL

kernel_name = 'hybrid_mla_swa_mem_encoder'


def rmsnorm(x, g):
    xf = x.astype(jnp.float32)
    y = xf * lax.rsqrt(jnp.mean(xf * xf, axis=-1, keepdims=True) + EPS)
    return (y * g.astype(jnp.float32)).astype(x.dtype)


def rope_tables(seq, dim):
    inv_freq = 1.0 / (ROPE_THETA ** (jnp.arange(0, dim, 2, dtype=jnp.float32) / dim))
    ang = jnp.arange(seq, dtype=jnp.float32)[:, None] * inv_freq[None, :]
    return jnp.cos(ang), jnp.sin(ang)


def apply_rope(x, cos, sin):
    xf = x.astype(jnp.float32)
    x1, x2 = jnp.split(xf, 2, axis=-1)
    c = cos[None, :, None, :]
    s = sin[None, :, None, :]
    return jnp.concatenate([x1 * c - x2 * s, x2 * c + x1 * s], axis=-1).astype(x.dtype)


def dense_attention_blocked(q, k, v, scale):
    b, s, h, dq = q.shape
    dv = v.shape[-1]
    nb = s // Q_BLOCK
    qb = q.reshape(b, nb, Q_BLOCK, h, dq).transpose(1, 0, 2, 3, 4)

    def one_block(qi):
        sc = jnp.einsum('bqhd,bkhd->bhqk', qi, k, preferred_element_type=jnp.float32) * scale
        p = jax.nn.softmax(sc, axis=-1)
        return jnp.einsum('bhqk,bkhd->bqhd', p.astype(v.dtype), v)

    o = lax.map(one_block, qb)
    return o.transpose(1, 0, 2, 3, 4).reshape(b, s, h * dv)


def mla_branch(c_q, c_kv, k_rope, q_a_norm, w_q_b, kv_a_norm, w_kv_b, g_q, g_k):
    b, s, _ = c_q.shape
    q = (rmsnorm(c_q, q_a_norm) @ w_q_b).reshape(b, s, MLA_HEADS, MLA_QK)
    kv = (rmsnorm(c_kv, kv_a_norm) @ w_kv_b).reshape(b, s, MLA_HEADS, MLA_NOPE + MLA_V)
    k_nope, v = kv[..., :MLA_NOPE], kv[..., MLA_NOPE:]
    k = jnp.concatenate([k_nope, jnp.broadcast_to(k_rope[:, :, None, :], (b, s, MLA_HEADS, MLA_ROPE))], axis=-1)
    q = rmsnorm(q, g_q)
    k = rmsnorm(k, g_k)
    cos, sin = rope_tables(s, MLA_ROPE)
    q = jnp.concatenate([q[..., :MLA_NOPE], apply_rope(q[..., MLA_NOPE:], cos, sin)], axis=-1)
    k = jnp.concatenate([k[..., :MLA_NOPE], apply_rope(k[..., MLA_NOPE:], cos, sin)], axis=-1)
    return dense_attention_blocked(q, k, v, MLA_QK ** -0.5)


def swa_branch(q, k, v, g_q, g_k, sink):
    b, s, _ = q.shape
    grp = SWA_HEADS // SWA_KV_HEADS
    nb = s // BLOCK
    q = rmsnorm(q.reshape(b, s, SWA_HEADS, SWA_HD), g_q)
    k = rmsnorm(k.reshape(b, s, SWA_KV_HEADS, SWA_HD), g_k)
    v = v.reshape(b, s, SWA_KV_HEADS, SWA_HD)
    cos, sin = rope_tables(s, SWA_HD)
    q = apply_rope(q, cos, sin)
    k = apply_rope(k, cos, sin)

    def windows(t):
        tp = jnp.pad(t, ((0, 0), (BLOCK, BLOCK), (0, 0), (0, 0))).reshape(b, nb + 2, BLOCK, SWA_KV_HEADS, SWA_HD)
        return jnp.concatenate([tp[:, :-2], tp[:, 1:-1], tp[:, 2:]], axis=2)

    kw, vw = windows(k), windows(v)
    qb = q.reshape(b, nb, BLOCK, SWA_KV_HEADS, grp, SWA_HD)
    sc = jnp.einsum('bnqhgd,bnkhd->bnhgqk', qb, kw, preferred_element_type=jnp.float32) * (SWA_HD ** -0.5)
    qpos = jnp.arange(nb)[:, None, None] * BLOCK + jnp.arange(BLOCK)[None, :, None]
    kpos = jnp.arange(nb)[:, None, None] * BLOCK - BLOCK + jnp.arange(3 * BLOCK)[None, None, :]
    mask = (jnp.abs(qpos - kpos) <= WINDOW) & (kpos >= 0) & (kpos < s)
    sc = jnp.where(mask[None, :, None, None], sc, -jnp.inf)
    sk = sink.astype(jnp.float32).reshape(SWA_KV_HEADS, grp)[None, None, :, :, None, None]
    m = jnp.maximum(jnp.max(sc, axis=-1, keepdims=True), sk)
    p = jnp.exp(sc - m)
    p = p / (jnp.sum(p, axis=-1, keepdims=True) + jnp.exp(sk - m))
    o = jnp.einsum('bnhgqk,bnkhd->bnqhgd', p.astype(vw.dtype), vw)
    return o.reshape(b, s, SWA_HEADS * SWA_HD)


def mem_branch(q, mem_n, w_mem_kv, g_q, g_k):
    b, s, _ = q.shape
    n_mem = mem_n.shape[1]
    q = rmsnorm(q.reshape(b, s, MEM_HEADS, MEM_HD), g_q)
    kv = (mem_n @ w_mem_kv).reshape(b, n_mem, 2, MEM_HEADS, MEM_HD)
    k = rmsnorm(kv[:, :, 0], g_k)
    v = kv[:, :, 1]
    sc = jnp.einsum('bshd,bmhd->bhsm', q, k, preferred_element_type=jnp.float32) * (MEM_HD ** -0.5)
    p = jax.nn.softmax(sc, axis=-1)
    o = jnp.einsum('bhsm,bmhd->bshd', p.astype(v.dtype), v)
    return o.reshape(b, s, MEM_HEADS * MEM_HD)


def depthwise_conv_centred(u, w, bias):
    c = u.shape[-1]
    y = lax.conv_general_dilated(u, w[:, None, :].astype(u.dtype), window_strides=(1,),
                                 padding=((CONV_W // 2, CONV_W // 2),),
                                 dimension_numbers=('NWC', 'WIO', 'NWC'), feature_group_count=c)
    return y + bias.astype(u.dtype)


def encoder_layer(x, mem, g_mix, g_mem, w_in, q_a_norm, w_q_b, kv_a_norm, w_kv_b, g_q_mla, g_k_mla,
                  g_q_swa, g_k_swa, swa_sink, w_mem_kv, g_q_mem, g_k_mem, w_o_mla, w_o_swa, w_o_mem,
                  w_out, g_ffn, w_up, conv_w, conv_b, w_down):
    b, s, d = x.shape
    h = rmsnorm(x, g_mix)
    z = h @ w_in
    offs = np.cumsum(SPLITS)[:-1].tolist()
    c_q, c_kv, k_rope, q_s, k_s, v_s, q_m, gate_logits = jnp.split(z, offs, axis=-1)
    o_mla = mla_branch(c_q, c_kv, k_rope, q_a_norm, w_q_b, kv_a_norm, w_kv_b, g_q_mla, g_k_mla)
    o_swa = swa_branch(q_s, k_s, v_s, g_q_swa, g_k_swa, swa_sink)
    o_mem = mem_branch(q_m, rmsnorm(mem, g_mem), w_mem_kv, g_q_mem, g_k_mem)
    gates = jax.nn.sigmoid(gate_logits.astype(jnp.float32)).reshape(b, s, N_BRANCH, d)
    merged = (gates[:, :, 0] * (o_mla @ w_o_mla) + gates[:, :, 1] * (o_swa @ w_o_swa)
              + gates[:, :, 2] * (o_mem @ w_o_mem)).astype(x.dtype)
    x = x + merged @ w_out
    h2 = rmsnorm(x, g_ffn)
    u = depthwise_conv_centred(h2 @ w_up, conv_w, conv_b)
    a, val = jnp.split(u, 2, axis=-1)
    return x + (jax.nn.silu(a) * val) @ w_down


def setup_inputs(seed: int = 0) -> dict:
    key = jax.random.key(seed)
    ks = jax.random.split(key, 32)
    L = DEPTH

    def dense(k, shape, fan_in):
        return jax.random.normal(k, shape, jnp.float32) * (fan_in ** -0.5)

    def gain(k, shape):
        return 1.0 + 0.02 * jax.random.normal(k, shape, jnp.float32)

    return {
        'x_prompt': jax.random.normal(ks[0], (BATCH, SEQ, D_MODEL), jnp.float32),
        'x_sample': jax.random.normal(ks[1], (DEC_BATCH, DEC_SEQ, D_MODEL), jnp.float32),
        'mem_prompt': jax.random.normal(ks[2], (BATCH, N_MEM, D_MODEL), jnp.float32),
        'mem_sample': jax.random.normal(ks[3], (DEC_BATCH, N_MEM, D_MODEL), jnp.float32),
        'g_mix': gain(ks[4], (L, D_MODEL)),
        'g_mem': gain(ks[5], (L, D_MODEL)),
        'w_in': dense(ks[6], (L, D_MODEL, D_IN), D_MODEL),
        'q_a_norm': gain(ks[7], (L, Q_LORA)),
        'w_q_b': dense(ks[8], (L, Q_LORA, MLA_HEADS * MLA_QK), Q_LORA),
        'kv_a_norm': gain(ks[9], (L, KV_LORA)),
        'w_kv_b': dense(ks[10], (L, KV_LORA, MLA_HEADS * (MLA_NOPE + MLA_V)), KV_LORA),
        'g_q_mla': gain(ks[11], (L, MLA_QK)),
        'g_k_mla': gain(ks[12], (L, MLA_QK)),
        'g_q_swa': gain(ks[13], (L, SWA_HD)),
        'g_k_swa': gain(ks[14], (L, SWA_HD)),
        'swa_sink': 0.5 * jax.random.normal(ks[15], (L, SWA_HEADS), jnp.float32),
        'w_mem_kv': dense(ks[16], (L, D_MODEL, 2 * MEM_HEADS * MEM_HD), D_MODEL),
        'g_q_mem': gain(ks[17], (L, MEM_HD)),
        'g_k_mem': gain(ks[18], (L, MEM_HD)),
        'w_o_mla': dense(ks[19], (L, MLA_HEADS * MLA_V, D_MODEL), MLA_HEADS * MLA_V),
        'w_o_swa': dense(ks[20], (L, SWA_HEADS * SWA_HD, D_MODEL), SWA_HEADS * SWA_HD),
        'w_o_mem': dense(ks[21], (L, MEM_HEADS * MEM_HD, D_MODEL), MEM_HEADS * MEM_HD),
        'w_out': dense(ks[22], (L, D_MODEL, D_MODEL), D_MODEL),
        'g_ffn': gain(ks[23], (L, D_MODEL)),
        'w_up': dense(ks[24], (L, D_MODEL, 2 * D_FF), D_MODEL),
        'conv_w': dense(ks[25], (L, CONV_W, 2 * D_FF), CONV_W),
        'conv_b': 0.02 * jax.random.normal(ks[26], (L, 2 * D_FF), jnp.float32),
        'w_down': dense(ks[27], (L, D_FF, D_MODEL), D_FF),
    }


def reference(x_prompt, x_sample, mem_prompt, mem_sample, g_mix, g_mem, w_in, q_a_norm, w_q_b,
              kv_a_norm, w_kv_b, g_q_mla, g_k_mla, g_q_swa, g_k_swa, swa_sink, w_mem_kv, g_q_mem,
              g_k_mem, w_o_mla, w_o_swa, w_o_mem, w_out, g_ffn, w_up, conv_w, conv_b, w_down):
    weights = (g_mix, g_mem, w_in, q_a_norm, w_q_b, kv_a_norm, w_kv_b, g_q_mla, g_k_mla, g_q_swa,
               g_k_swa, swa_sink, w_mem_kv, g_q_mem, g_k_mem, w_o_mla, w_o_swa, w_o_mem, w_out,
               g_ffn, w_up, conv_w, conv_b, w_down)
    y_prompt = x_prompt
    y_sample = x_sample
    for layer in range(DEPTH):
        lw = [w[layer] for w in weights]
        y_prompt = encoder_layer(y_prompt, mem_prompt, *lw)
        y_sample = encoder_layer(y_sample, mem_sample, *lw)
    return (y_prompt, y_sample)
```

```python
import functools
import math

import jax
import jax.numpy as jnp
from jax import lax
from jax.experimental import pallas as pl
from jax.experimental.pallas import tpu as pltpu

D_MODEL = 1024
N_MEM = 256
EPS = 1e-6
ROPE_THETA = 10000.0
MLA_HEADS = 8
MLA_NOPE = 64
MLA_ROPE = 32
MLA_V = 64
MLA_QK = MLA_NOPE + MLA_ROPE
Q_LORA = 384
KV_LORA = 256
SWA_HEADS = 8
SWA_KV_HEADS = 2
SWA_GROUP = SWA_HEADS // SWA_KV_HEADS
SWA_HD = 64
WINDOW = 128
BLOCK = 128
MEM_HEADS = 4
MEM_HD = 128
N_BRANCH = 3
D_FF = 2816
SPLITS = (Q_LORA, KV_LORA, MLA_ROPE, SWA_HEADS * SWA_HD, SWA_KV_HEADS * SWA_HD,
          SWA_KV_HEADS * SWA_HD, MEM_HEADS * MEM_HD, N_BRANCH * D_MODEL)

LANE = 128
MLA_HP = 128
MLA_VROWS = 80
LOG2E = math.log2(math.e)
NEG_BIG = -1e30
VMEM_LIMIT = 56 * 1024 * 1024

BF16 = jnp.bfloat16
F32 = jnp.float32


def _dot(a, b):
    return jnp.dot(a, b, preferred_element_type=F32)


def _dot_nt(a, b):
    return lax.dot_general(a, b, (((1,), (1,)), ((), ())), preferred_element_type=F32)


def _dot_tn(a, b):
    return lax.dot_general(a, b, (((0,), (0,)), ((), ())), preferred_element_type=F32)


def _const_spec(shape):
    nd = len(shape)
    return pl.BlockSpec(shape, lambda *_: (0,) * nd, pipeline_mode=pl.Buffered(1))


def _params(semantics):
    return pltpu.CompilerParams(dimension_semantics=semantics, vmem_limit_bytes=VMEM_LIMIT)


def _proj_kernel(x_ref, gmix_ref, wa_ref, wbt_ref, qan_ref, wqt_ref, kvan_ref, wk_ref, wvt_ref,
                 gq_mla_ref, gk_mla_ref, gq_swa_ref, gk_swa_ref, gq_mem_ref,
                 ct16_ref, st16_ref, ck_ref, sak_ref, sbk_ref,
                 ct32_ref, st32_ref, cs_ref, sas_ref, sbs_ref,
                 qt_mla_ref, k_mla_ref, vt_mla_ref, qt_swa_ref, k_swa_ref, vt_swa_ref, q_mem_ref):
    x = x_ref[...]
    tm = x.shape[0]
    h = x * lax.rsqrt(jnp.mean(x * x, axis=-1, keepdims=True) + EPS) * gmix_ref[...]
    hb = h.astype(BF16)
    z = _dot(hb, wa_ref[...])
    c_q = z[:, 0:384]
    c_kv = z[:, 384:640]
    k_s = z[:, 640:768]
    q_m = z[:, 768:1280]
    k_rope = z[:, 1280:1408]

    c_qn = (c_q * lax.rsqrt(jnp.mean(c_q * c_q, axis=-1, keepdims=True) + EPS) * qan_ref[...]).astype(BF16)
    qt = _dot_nt(wqt_ref[...], c_qn)
    ct = ct16_ref[...]
    st = st16_ref[...]
    gq = gq_mla_ref[...]
    for hd in range(MLA_HEADS):
        blk = qt[hd * MLA_HP:(hd + 1) * MLA_HP]
        ss = jnp.sum(blk * blk, axis=0, keepdims=True)
        y = blk * lax.rsqrt(ss * (1.0 / MLA_QK) + EPS) * gq
        x1 = y[64:80]
        x2 = y[80:96]
        out = jnp.concatenate([y[0:64], x1 * ct - x2 * st, x2 * ct + x1 * st, y[96:128]], axis=0)
        qt_mla_ref[hd * MLA_HP:(hd + 1) * MLA_HP, :] = out.astype(BF16)

    c_kvn = (c_kv * lax.rsqrt(jnp.mean(c_kv * c_kv, axis=-1, keepdims=True) + EPS) * kvan_ref[...]).astype(BF16)
    kin = jnp.concatenate([c_kvn, k_rope.astype(BF16)], axis=1)
    kpre = _dot(kin, wk_ref[...])
    ck = ck_ref[...]
    sak = sak_ref[...]
    sbk = sbk_ref[...]
    gk = gk_mla_ref[...]
    for hd in range(MLA_HEADS):
        blk = kpre[:, hd * MLA_HP:(hd + 1) * MLA_HP]
        ss = jnp.sum(blk * blk, axis=-1, keepdims=True)
        y = blk * lax.rsqrt(ss * (1.0 / MLA_QK) + EPS) * gk
        out = y * ck + pltpu.roll(y, 112, 1) * sak + pltpu.roll(y, 16, 1) * sbk
        k_mla_ref[:, hd * MLA_HP:(hd + 1) * MLA_HP] = out.astype(BF16)
    vt = _dot_nt(wvt_ref[...], c_kvn)
    ones = jnp.ones((MLA_VROWS - MLA_V, tm), BF16)
    for hd in range(MLA_HEADS):
        vt_mla_ref[hd * MLA_VROWS:hd * MLA_VROWS + MLA_V, :] = vt[hd * MLA_V:(hd + 1) * MLA_V].astype(BF16)
        vt_mla_ref[hd * MLA_VROWS + MLA_V:(hd + 1) * MLA_VROWS, :] = ones

    bt = _dot_nt(wbt_ref[...], hb)
    c32 = ct32_ref[...]
    s32 = st32_ref[...]
    gqs = gq_swa_ref[...]
    for hd in range(SWA_HEADS):
        blk = bt[hd * SWA_HD:(hd + 1) * SWA_HD]
        ss = jnp.sum(blk * blk, axis=0, keepdims=True)
        y = blk * lax.rsqrt(ss * (1.0 / SWA_HD) + EPS) * gqs
        x1 = y[0:32]
        x2 = y[32:64]
        out = jnp.concatenate([x1 * c32 - x2 * s32, x2 * c32 + x1 * s32], axis=0)
        qt_swa_ref[hd * SWA_HD:(hd + 1) * SWA_HD, :] = out.astype(BF16)
    vt_swa_ref[...] = bt[SWA_HEADS * SWA_HD:].astype(BF16)
    lane = lax.broadcasted_iota(jnp.int32, (tm, LANE), 1)
    lo = lane < SWA_HD
    sq = k_s * k_s
    ss_lo = jnp.sum(jnp.where(lo, sq, 0.0), axis=-1, keepdims=True)
    ss_hi = jnp.sum(jnp.where(lo, 0.0, sq), axis=-1, keepdims=True)
    rstd = jnp.where(lo, lax.rsqrt(ss_lo * (1.0 / SWA_HD) + EPS), lax.rsqrt(ss_hi * (1.0 / SWA_HD) + EPS))
    y = k_s * rstd * gk_swa_ref[...]
    out = y * cs_ref[...] + pltpu.roll(y, 96, 1) * sas_ref[...] + pltpu.roll(y, 32, 1) * sbs_ref[...]
    k_swa_ref[...] = out.astype(BF16)

    gqm = gq_mem_ref[...]
    for hd in range(MEM_HEADS):
        blk = q_m[:, hd * MEM_HD:(hd + 1) * MEM_HD]
        ss = jnp.sum(blk * blk, axis=-1, keepdims=True)
        y = blk * lax.rsqrt(ss * (1.0 / MEM_HD) + EPS) * gqm
        q_mem_ref[:, hd * MEM_HD:(hd + 1) * MEM_HD] = y.astype(BF16)


def _proj(x, w, tabs, tm):
    b, s, d = x.shape
    grid = (b, s // tm)
    tok = lambda width: pl.BlockSpec((None, tm, width), lambda i, j: (i, j, 0))
    tok_t = lambda rows: pl.BlockSpec((None, rows, tm), lambda i, j: (i, 0, j))
    tab = lambda width: pl.BlockSpec((tm, width), lambda i, j: (j, 0))
    tab_t = lambda rows: pl.BlockSpec((rows, tm), lambda i, j: (0, j))
    consts = [w['g_mix'], w['wa'], w['wbt'], w['q_a_norm'], w['wqt'], w['kv_a_norm'], w['wk'], w['wvt'],
              w['gq_mla'], w['gk_mla'], w['gq_swa'], w['gk_swa'], w['gq_mem']]
    in_specs = [tok(d)] + [_const_spec(c.shape) for c in consts] + [
        tab_t(16), tab_t(16), tab(LANE), tab(LANE), tab(LANE),
        tab_t(32), tab_t(32), tab(LANE), tab(LANE), tab(LANE)]
    out_shape = (
        jax.ShapeDtypeStruct((b, MLA_HEADS * MLA_HP, s), BF16),
        jax.ShapeDtypeStruct((b, s, MLA_HEADS * MLA_HP), BF16),
        jax.ShapeDtypeStruct((b, MLA_HEADS * MLA_VROWS, s), BF16),
        jax.ShapeDtypeStruct((b, SWA_HEADS * SWA_HD, s), BF16),
        jax.ShapeDtypeStruct((b, s, SWA_KV_HEADS * SWA_HD), BF16),
        jax.ShapeDtypeStruct((b, SWA_KV_HEADS * SWA_HD, s), BF16),
        jax.ShapeDtypeStruct((b, s, MEM_HEADS * MEM_HD), BF16),
    )
    out_specs = (tok_t(MLA_HEADS * MLA_HP), tok(MLA_HEADS * MLA_HP), tok_t(MLA_HEADS * MLA_VROWS),
                 tok_t(SWA_HEADS * SWA_HD), tok(SWA_KV_HEADS * SWA_HD), tok_t(SWA_KV_HEADS * SWA_HD),
                 tok(MEM_HEADS * MEM_HD))
    return pl.pallas_call(
        _proj_kernel, grid=grid, in_specs=in_specs, out_specs=out_specs, out_shape=out_shape,
        compiler_params=_params(("parallel", "parallel")), name="proj",
    )(x, *consts, tabs['ct16'], tabs['st16'], tabs['ck'], tabs['sak'], tabs['sbk'],
      tabs['ct32'], tabs['st32'], tabs['cs'], tabs['sas'], tabs['sbs'])


def _mem_kv_kernel(mem_ref, gmem_ref, wkt_ref, wv_ref, gk_ref, kt_ref, v_ref):
    m = mem_ref[...]
    mn = (m * lax.rsqrt(jnp.mean(m * m, axis=-1, keepdims=True) + EPS) * gmem_ref[...]).astype(BF16)
    kt = _dot_nt(wkt_ref[...], mn)
    gk = gk_ref[...]
    for hd in range(MEM_HEADS):
        blk = kt[hd * MEM_HD:(hd + 1) * MEM_HD]
        ss = jnp.sum(blk * blk, axis=0, keepdims=True)
        kt_ref[hd * MEM_HD:(hd + 1) * MEM_HD, :] = (blk * lax.rsqrt(ss * (1.0 / MEM_HD) + EPS) * gk).astype(BF16)
    v_ref[...] = _dot(mn, wv_ref[...]).astype(BF16)


def _mem_kv(mem, w):
    b, n, d = mem.shape
    hw = MEM_HEADS * MEM_HD
    consts = [w['g_mem'], w['wmkt'], w['wmv'], w['gk_mem']]
    return pl.pallas_call(
        _mem_kv_kernel, grid=(b,),
        in_specs=[pl.BlockSpec((None, n, d), lambda i: (i, 0, 0))] + [_const_spec(c.shape) for c in consts],
        out_specs=(pl.BlockSpec((None, hw, n), lambda i: (i, 0, 0)), pl.BlockSpec((None, n, hw), lambda i: (i, 0, 0))),
        out_shape=(jax.ShapeDtypeStruct((b, hw, n), BF16), jax.ShapeDtypeStruct((b, n, hw), BF16)),
        compiler_params=_params(("parallel",)), name="mem_kv",
    )(mem, *consts)


def _mla_kernel(qt_ref, k_ref, vt_ref, ot_ref, *, tk):
    s = k_ref.shape[0]
    tq = qt_ref.shape[1]
    qt = qt_ref[...]

    def step(c, carry):
        m_prev, acc = carry
        off = pl.multiple_of(c * tk, tk)
        st = _dot(k_ref[pl.ds(off, tk), :], qt)
        m_new = jnp.maximum(m_prev, jnp.max(st, axis=0, keepdims=True))
        alpha = jnp.exp2(m_prev - m_new)
        pt = jnp.exp2(st - m_new).astype(BF16)
        acc = alpha * acc + _dot(vt_ref[:, pl.ds(off, tk)], pt)
        return m_new, acc

    m0 = jnp.full((1, tq), NEG_BIG, F32)
    acc0 = jnp.zeros((MLA_VROWS, tq), F32)
    _, acc = lax.fori_loop(0, s // tk, step, (m0, acc0))
    ot_ref[...] = (acc[0:MLA_V] / acc[MLA_V:MLA_V + 1]).astype(BF16)


def _mla_attn(qt, k, vt, tq, tk):
    b, _, s = qt.shape
    grid = (b, MLA_HEADS, s // tq)
    return pl.pallas_call(
        functools.partial(_mla_kernel, tk=tk), grid=grid,
        in_specs=[pl.BlockSpec((None, MLA_HP, tq), lambda i, h, j: (i, h, j)),
                  pl.BlockSpec((None, s, MLA_HP), lambda i, h, j: (i, 0, h)),
                  pl.BlockSpec((None, MLA_VROWS, s), lambda i, h, j: (i, h, 0))],
        out_specs=pl.BlockSpec((None, MLA_V, tq), lambda i, h, j: (i, h, j)),
        out_shape=jax.ShapeDtypeStruct((b, MLA_HEADS * MLA_V, s), BF16),
        compiler_params=_params(("parallel", "parallel", "arbitrary")), name="mla_attn",
    )(qt, k, vt)


def _swa_kernel(qt_ref, k_ref, vt_ref, sink_ref, ot_ref):
    s = k_ref.shape[0]
    t = qt_ref.shape[1]
    span = 3 * BLOCK
    gw = SWA_GROUP * BLOCK
    zeros = jnp.zeros((SWA_HD, gw), BF16)
    for blk in range(t // BLOCK):
        n = pl.program_id(1) * (t // BLOCK) + blk
        start = pl.multiple_of(jnp.clip((n - 1) * BLOCK, 0, s - span), BLOCK)
        kwin = k_ref[pl.ds(start, span), :]
        vwin = vt_ref[:, pl.ds(start, span)]
        kpos = start + lax.broadcasted_iota(jnp.int32, (span, gw), 0)
        qpos = n * BLOCK + (lax.broadcasted_iota(jnp.int32, (span, gw), 1) & (BLOCK - 1))
        mask = jnp.abs(qpos - kpos) <= WINDOW
        for g in range(SWA_KV_HEADS):
            qg = jnp.concatenate(
                [qt_ref[(g * SWA_GROUP + j) * SWA_HD:(g * SWA_GROUP + j + 1) * SWA_HD, blk * BLOCK:(blk + 1) * BLOCK]
                 for j in range(SWA_GROUP)], axis=1)
            qpad = jnp.concatenate([qg, zeros] if g == 0 else [zeros, qg], axis=0)
            st = jnp.where(mask, _dot(kwin, qpad), NEG_BIG)
            sk = sink_ref[g:g + 1, :]
            m = jnp.maximum(jnp.max(st, axis=0, keepdims=True), sk)
            p = jnp.exp2(st - m)
            denom = jnp.sum(p, axis=0, keepdims=True) + jnp.exp2(sk - m)
            ot = _dot(vwin[g * SWA_HD:(g + 1) * SWA_HD], p.astype(BF16)) / denom
            for j in range(SWA_GROUP):
                hd = g * SWA_GROUP + j
                ot_ref[hd * SWA_HD:(hd + 1) * SWA_HD, blk * BLOCK:(blk + 1) * BLOCK] = (
                    ot[:, j * BLOCK:(j + 1) * BLOCK].astype(BF16))


def _swa_attn(qt, k, vt, sink_rows, t):
    b, hw, s = qt.shape
    kvw = SWA_KV_HEADS * SWA_HD
    return pl.pallas_call(
        _swa_kernel, grid=(b, s // t),
        in_specs=[pl.BlockSpec((None, hw, t), lambda i, j: (i, 0, j)),
                  pl.BlockSpec((None, s, kvw), lambda i, j: (i, 0, 0)),
                  pl.BlockSpec((None, kvw, s), lambda i, j: (i, 0, 0)),
                  _const_spec(sink_rows.shape)],
        out_specs=pl.BlockSpec((None, hw, t), lambda i, j: (i, 0, j)),
        out_shape=jax.ShapeDtypeStruct((b, hw, s), BF16),
        compiler_params=_params(("parallel", "arbitrary")), name="swa_attn",
    )(qt, k, vt, sink_rows)


def _sigmoid(v):
    return 1.0 / (1.0 + jnp.exp(-v))


def _merge_kernel(x_ref, ot_mla_ref, ot_swa_ref, q_mem_ref, kt_mem_ref, v_mem_ref, gmix_ref, wg_ref,
                  wo_mla_ref, wo_swa_ref, wo_mem_ref, wout_ref, y_ref):
    x = x_ref[...]
    hb = (x * lax.rsqrt(jnp.mean(x * x, axis=-1, keepdims=True) + EPS) * gmix_ref[...]).astype(BF16)

    o_heads = []
    for hd in range(MEM_HEADS):
        q = q_mem_ref[:, hd * MEM_HD:(hd + 1) * MEM_HD]
        sc = _dot(q, kt_mem_ref[hd * MEM_HD:(hd + 1) * MEM_HD, :])
        p = jnp.exp2(sc - jnp.max(sc, axis=-1, keepdims=True))
        l = jnp.sum(p, axis=-1, keepdims=True)
        o_heads.append(_dot(p.astype(BF16), v_mem_ref[:, hd * MEM_HD:(hd + 1) * MEM_HD]) / l)
    o_mem = jnp.concatenate(o_heads, axis=1).astype(BF16)

    merged = _sigmoid(_dot(hb, wg_ref[:, 0:D_MODEL])) * _dot_tn(ot_mla_ref[...], wo_mla_ref[...])
    merged += _sigmoid(_dot(hb, wg_ref[:, D_MODEL:2 * D_MODEL])) * _dot_tn(ot_swa_ref[...], wo_swa_ref[...])
    merged += _sigmoid(_dot(hb, wg_ref[:, 2 * D_MODEL:3 * D_MODEL])) * _dot(o_mem, wo_mem_ref[...])
    y_ref[...] = x + _dot(merged.astype(BF16), wout_ref[...])


def _merge(x, ot_mla, ot_swa, q_mem, kt_mem, v_mem, w, tm):
    b, s, d = x.shape
    n_mem = v_mem.shape[1]
    hw = MEM_HEADS * MEM_HD
    tok = lambda width: pl.BlockSpec((None, tm, width), lambda i, j: (i, j, 0))
    tok_t = lambda rows: pl.BlockSpec((None, rows, tm), lambda i, j: (i, 0, j))
    consts = [w['g_mix'], w['wg'], w['wo_mla'], w['wo_swa'], w['wo_mem'], w['wout']]
    return pl.pallas_call(
        _merge_kernel, grid=(b, s // tm),
        in_specs=[tok(d), tok_t(MLA_HEADS * MLA_V), tok_t(SWA_HEADS * SWA_HD), tok(hw),
                  pl.BlockSpec((None, hw, n_mem), lambda i, j: (i, 0, 0)),
                  pl.BlockSpec((None, n_mem, hw), lambda i, j: (i, 0, 0))] + [_const_spec(c.shape) for c in consts],
        out_specs=tok(d), out_shape=jax.ShapeDtypeStruct((b, s, d), F32),
        compiler_params=_params(("parallel", "parallel")), name="merge",
    )(x, ot_mla, ot_swa, q_mem, kt_mem, v_mem, *consts)


HALO = 8


def _ffn_kernel(x_ref, prev_ref, next_ref, gffn_ref, wup_ref, cw_ref, cb_ref, wdown_ref, y_ref):
    j = pl.program_id(1)
    nj = pl.num_programs(1)
    x = x_ref[...]
    tm = x.shape[0]
    g = gffn_ref[...]

    def norm(v):
        return v * lax.rsqrt(jnp.mean(v * v, axis=-1, keepdims=True) + EPS) * g

    hp = jnp.where(j > 0, norm(prev_ref[...]), 0.0)
    hn = jnp.where(j < nj - 1, norm(next_ref[...]), 0.0)
    hext = jnp.concatenate([hp, norm(x), hn], axis=0).astype(BF16)
    u = _dot(hext, wup_ref[...])
    cw = cw_ref[...]
    conv = (u[HALO - 1:HALO - 1 + tm] * cw[0:1] + u[HALO:HALO + tm] * cw[1:2]
            + u[HALO + 1:HALO + 1 + tm] * cw[2:3] + cb_ref[...])
    a = conv[:, :D_FF]
    val = conv[:, D_FF:]
    act = (a * _sigmoid(a) * val).astype(BF16)
    y_ref[...] = x + _dot(act, wdown_ref[...])


def _ffn(x, w, tm):
    b, s, d = x.shape
    nh = tm // HALO
    last = s // HALO - 1
    consts = [w['g_ffn'], w['wup'], w['conv_w'], w['conv_b'], w['wdown']]
    return pl.pallas_call(
        _ffn_kernel, grid=(b, s // tm),
        in_specs=[pl.BlockSpec((None, tm, d), lambda i, j: (i, j, 0)),
                  pl.BlockSpec((None, HALO, d), lambda i, j: (i, jnp.maximum(j * nh - 1, 0), 0)),
                  pl.BlockSpec((None, HALO, d), lambda i, j: (i, jnp.minimum((j + 1) * nh, last), 0))]
                 + [_const_spec(c.shape) for c in consts],
        out_specs=pl.BlockSpec((None, tm, d), lambda i, j: (i, j, 0)),
        out_shape=jax.ShapeDtypeStruct((b, s, d), F32),
        compiler_params=_params(("parallel", "parallel")), name="ffn",
    )(x, x, x, *consts)


def _prep_weights(g_mix, g_mem, w_in, q_a_norm, w_q_b, kv_a_norm, w_kv_b, g_q_mla, g_k_mla, g_q_swa, g_k_swa,
                  swa_sink, w_mem_kv, g_q_mem, g_k_mem, w_o_mla, w_o_swa, w_o_mem, w_out, g_ffn, w_up, conv_w,
                  conv_b, w_down):
    offs = [0]
    for sp in SPLITS:
        offs.append(offs[-1] + sp)
    w_cq, w_ckv, w_kr, w_qs, w_ks, w_vs, w_qm, w_gate = (w_in[:, offs[i]:offs[i + 1]] for i in range(8))
    row = lambda v: v.reshape(1, -1).astype(F32)
    col = lambda v: v.reshape(-1, 1).astype(F32)
    w = {}
    w['g_mix'] = row(g_mix)
    w['g_mem'] = row(g_mem)
    w['g_ffn'] = row(g_ffn)
    w['q_a_norm'] = row(q_a_norm)
    w['kv_a_norm'] = row(kv_a_norm)
    w['wa'] = jnp.concatenate([w_cq, w_ckv, w_ks, w_qm, jnp.pad(w_kr, ((0, 0), (0, LANE - MLA_ROPE)))], axis=1).astype(BF16)
    w['wbt'] = jnp.concatenate([w_qs, w_vs], axis=1).T.astype(BF16)
    w['wg'] = w_gate.astype(BF16)
    wq = jnp.pad(w_q_b.reshape(Q_LORA, MLA_HEADS, MLA_QK), ((0, 0), (0, 0), (0, MLA_HP - MLA_QK)))
    w['wqt'] = wq.reshape(Q_LORA, MLA_HEADS * MLA_HP).T.astype(BF16)
    wkv = w_kv_b.reshape(KV_LORA, MLA_HEADS, MLA_NOPE + MLA_V)
    wk_nope = jnp.pad(wkv[:, :, :MLA_NOPE], ((0, 0), (0, 0), (0, MLA_HP - MLA_NOPE))).reshape(KV_LORA, -1)
    place = jnp.zeros((LANE, MLA_HP), F32).at[jnp.arange(MLA_ROPE), MLA_NOPE + jnp.arange(MLA_ROPE)].set(1.0)
    w['wk'] = jnp.concatenate([wk_nope, jnp.tile(place, (1, MLA_HEADS))], axis=0).astype(BF16)
    w['wvt'] = wkv[:, :, MLA_NOPE:].reshape(KV_LORA, MLA_HEADS * MLA_V).T.astype(BF16)
    sc_mla = MLA_QK ** -0.5 * LOG2E
    w['gq_mla'] = col(jnp.pad(g_q_mla * sc_mla, (0, MLA_HP - MLA_QK)))
    w['gk_mla'] = row(jnp.pad(g_k_mla, (0, MLA_HP - MLA_QK)))
    w['gq_swa'] = col(g_q_swa * (SWA_HD ** -0.5 * LOG2E))
    w['gk_swa'] = row(jnp.tile(g_k_swa, SWA_KV_HEADS))
    w['gq_mem'] = row(g_q_mem * (MEM_HD ** -0.5 * LOG2E))
    w['gk_mem'] = col(g_k_mem)
    w['sink_rows'] = jnp.repeat(swa_sink.astype(F32) * LOG2E, BLOCK).reshape(SWA_KV_HEADS, SWA_GROUP * BLOCK)
    hw = MEM_HEADS * MEM_HD
    w['wmkt'] = w_mem_kv[:, :hw].T.astype(BF16)
    w['wmv'] = w_mem_kv[:, hw:].astype(BF16)
    w['wo_mla'] = w_o_mla.astype(BF16)
    w['wo_swa'] = w_o_swa.astype(BF16)
    w['wo_mem'] = w_o_mem.astype(BF16)
    w['wout'] = w_out.astype(BF16)
    w['wup'] = w_up.astype(BF16)
    w['wdown'] = w_down.astype(BF16)
    w['conv_w'] = conv_w.astype(F32)
    w['conv_b'] = row(conv_b)
    return w


def _rope_tables(s):
    def cs(dim):
        inv = 1.0 / (ROPE_THETA ** (jnp.arange(0, dim, 2, dtype=F32) / dim))
        ang = jnp.arange(s, dtype=F32)[:, None] * inv[None, :]
        return jnp.cos(ang), jnp.sin(ang)

    c16, s16 = cs(MLA_ROPE)
    c32, s32 = cs(SWA_HD)
    z = lambda n: jnp.zeros((s, n), F32)
    t = {'ct16': c16.T, 'st16': s16.T, 'ct32': c32.T, 'st32': s32.T}
    t['ck'] = jnp.concatenate([jnp.ones((s, MLA_NOPE), F32), c16, c16, z(MLA_HP - MLA_QK)], axis=1)
    t['sak'] = jnp.concatenate([z(MLA_NOPE), -s16, z(16), z(MLA_HP - MLA_QK)], axis=1)
    t['sbk'] = jnp.concatenate([z(MLA_NOPE), z(16), s16, z(MLA_HP - MLA_QK)], axis=1)
    t['cs'] = jnp.concatenate([c32, c32] * SWA_KV_HEADS, axis=1)
    t['sas'] = jnp.concatenate([-s32, z(32)] * SWA_KV_HEADS, axis=1)
    t['sbs'] = jnp.concatenate([z(32), s32] * SWA_KV_HEADS, axis=1)
    return t


def _tiles(s):
    pick = lambda want: want if s % want == 0 else BLOCK
    return dict(proj=pick(256), mla_q=pick(256), mla_k=pick(2048), swa=pick(512), merge=pick(256), ffn=pick(256))


def _layer(x, mem, w):
    b, s, d = x.shape
    assert d == D_MODEL and s % BLOCK == 0 and s >= 3 * BLOCK
    ts = _tiles(s)
    tabs = _rope_tables(s)
    qt_mla, k_mla, vt_mla, qt_swa, k_swa, vt_swa, q_mem = _proj(x, w, tabs, ts['proj'])
    kt_mem, v_mem = _mem_kv(mem, w)
    ot_mla = _mla_attn(qt_mla, k_mla, vt_mla, ts['mla_q'], ts['mla_k'])
    ot_swa = _swa_attn(qt_swa, k_swa, vt_swa, w['sink_rows'], ts['swa'])
    x1 = _merge(x, ot_mla, ot_swa, q_mem, kt_mem, v_mem, w, ts['merge'])
    return _ffn(x1, w, ts['ffn'])


def kernel(x_prompt, x_sample, mem_prompt, mem_sample, g_mix, g_mem, w_in, q_a_norm, w_q_b, kv_a_norm, w_kv_b,
           g_q_mla, g_k_mla, g_q_swa, g_k_swa, swa_sink, w_mem_kv, g_q_mem, g_k_mem, w_o_mla, w_o_swa, w_o_mem,
           w_out, g_ffn, w_up, conv_w, conv_b, w_down):
    weights = (g_mix, g_mem, w_in, q_a_norm, w_q_b, kv_a_norm, w_kv_b, g_q_mla, g_k_mla, g_q_swa, g_k_swa,
               swa_sink, w_mem_kv, g_q_mem, g_k_mem, w_o_mla, w_o_swa, w_o_mem, w_out, g_ffn, w_up, conv_w,
               conv_b, w_down)
    depth = g_mix.shape[0]
    y_prompt, y_sample = x_prompt, x_sample
    for layer in range(depth):
        w = _prep_weights(*(p[layer] for p in weights))
        y_prompt = _layer(y_prompt, mem_prompt, w)
        y_sample = _layer(y_sample, mem_sample, w)
    return (y_prompt, y_sample)
```

```python
import functools
import math

import jax
import jax.numpy as jnp
from jax import lax
from jax.experimental import pallas as pl
from jax.experimental.pallas import tpu as pltpu

D_MODEL = 1024
N_MEM = 256
EPS = 1e-6
ROPE_THETA = 10000.0
MLA_HEADS = 8
MLA_NOPE = 64
MLA_ROPE = 32
MLA_V = 64
MLA_QK = MLA_NOPE + MLA_ROPE
Q_LORA = 384
KV_LORA = 256
SWA_HEADS = 8
SWA_KV_HEADS = 2
SWA_GROUP = SWA_HEADS // SWA_KV_HEADS
SWA_HD = 64
WINDOW = 128
BLOCK = 128
MEM_HEADS = 4
MEM_HD = 128
N_BRANCH = 3
D_FF = 2816
SPLITS = (Q_LORA, KV_LORA, MLA_ROPE, SWA_HEADS * SWA_HD, SWA_KV_HEADS * SWA_HD,
          SWA_KV_HEADS * SWA_HD, MEM_HEADS * MEM_HD, N_BRANCH * D_MODEL)

LANE = 128
MLA_HP = 128
MLA_VROWS = 80
LOG2E = math.log2(math.e)
NEG_BIG = -1e30
VMEM_LIMIT = 56 * 1024 * 1024

BF16 = jnp.bfloat16
F32 = jnp.float32


def _dot(a, b):
    return jnp.dot(a, b, preferred_element_type=F32)


def _dot_nt(a, b):
    return lax.dot_general(a, b, (((1,), (1,)), ((), ())), preferred_element_type=F32)


def _dot_tn(a, b):
    return lax.dot_general(a, b, (((0,), (0,)), ((), ())), preferred_element_type=F32)


def _const_spec(shape):
    nd = len(shape)
    return pl.BlockSpec(shape, lambda *_: (0,) * nd, pipeline_mode=pl.Buffered(1))


def _params(semantics):
    return pltpu.CompilerParams(dimension_semantics=semantics, vmem_limit_bytes=VMEM_LIMIT)


def _proj_kernel(x_ref, gmix_ref, wa_ref, wbt_ref, qan_ref, wqt_ref, kvan_ref, wk_ref, wvt_ref,
                 gq_mla_ref, gk_mla_ref, gq_swa_ref, gk_swa_ref, gq_mem_ref,
                 ct16_ref, st16_ref, ck_ref, sak_ref, sbk_ref,
                 ct32_ref, st32_ref, cs_ref, sas_ref, sbs_ref,
                 qt_mla_ref, k_mla_ref, vt_mla_ref, qt_swa_ref, k_swa_ref, vt_swa_ref, q_mem_ref):
    x = x_ref[...]
    tm = x.shape[0]
    h = x * lax.rsqrt(jnp.mean(x * x, axis=-1, keepdims=True) + EPS) * gmix_ref[...]
    hb = h.astype(BF16)
    z = _dot(hb, wa_ref[...])
    c_q = z[:, 0:384]
    c_kv = z[:, 384:640]
    k_s = z[:, 640:768]
    q_m = z[:, 768:1280]
    k_rope = z[:, 1280:1408]

    c_qn = (c_q * lax.rsqrt(jnp.mean(c_q * c_q, axis=-1, keepdims=True) + EPS) * qan_ref[...]).astype(BF16)
    qt = _dot_nt(wqt_ref[...], c_qn)
    ct = ct16_ref[...]
    st = st16_ref[...]
    gq = gq_mla_ref[...]
    for hd in range(MLA_HEADS):
        blk = qt[hd * MLA_HP:(hd + 1) * MLA_HP]
        ss = jnp.sum(blk * blk, axis=0, keepdims=True)
        y = blk * lax.rsqrt(ss * (1.0 / MLA_QK) + EPS) * gq
        x1 = y[64:80]
        x2 = y[80:96]
        out = jnp.concatenate([y[0:64], x1 * ct - x2 * st, x2 * ct + x1 * st, y[96:128]], axis=0)
        qt_mla_ref[hd * MLA_HP:(hd + 1) * MLA_HP, :] = out.astype(BF16)

    c_kvn = (c_kv * lax.rsqrt(jnp.mean(c_kv * c_kv, axis=-1, keepdims=True) + EPS) * kvan_ref[...]).astype(BF16)
    kin = jnp.concatenate([c_kvn, k_rope.astype(BF16)], axis=1)
    kpre = _dot(kin, wk_ref[...])
    ck = ck_ref[...]
    sak = sak_ref[...]
    sbk = sbk_ref[...]
    gk = gk_mla_ref[...]
    for hd in range(MLA_HEADS):
        blk = kpre[:, hd * MLA_HP:(hd + 1) * MLA_HP]
        ss = jnp.sum(blk * blk, axis=-1, keepdims=True)
        y = blk * lax.rsqrt(ss * (1.0 / MLA_QK) + EPS) * gk
        out = y * ck + pltpu.roll(y, 112, 1) * sak + pltpu.roll(y, 16, 1) * sbk
        k_mla_ref[:, hd * MLA_HP:(hd + 1) * MLA_HP] = out.astype(BF16)
    vt = _dot_nt(wvt_ref[...], c_kvn)
    ones = jnp.ones((MLA_VROWS - MLA_V, tm), BF16)
    for hd in range(MLA_HEADS):
        vt_mla_ref[hd * MLA_VROWS:hd * MLA_VROWS + MLA_V, :] = vt[hd * MLA_V:(hd + 1) * MLA_V].astype(BF16)
        vt_mla_ref[hd * MLA_VROWS + MLA_V:(hd + 1) * MLA_VROWS, :] = ones

    bt = _dot_nt(wbt_ref[...], hb)
    c32 = ct32_ref[...]
    s32 = st32_ref[...]
    gqs = gq_swa_ref[...]
    for hd in range(SWA_HEADS):
        blk = bt[hd * SWA_HD:(hd + 1) * SWA_HD]
        ss = jnp.sum(blk * blk, axis=0, keepdims=True)
        y = blk * lax.rsqrt(ss * (1.0 / SWA_HD) + EPS) * gqs
        x1 = y[0:32]
        x2 = y[32:64]
        out = jnp.concatenate([x1 * c32 - x2 * s32, x2 * c32 + x1 * s32], axis=0)
        qt_swa_ref[hd * SWA_HD:(hd + 1) * SWA_HD, :] = out.astype(BF16)
    vt_swa_ref[...] = bt[SWA_HEADS * SWA_HD:].astype(BF16)
    lane = lax.broadcasted_iota(jnp.int32, (tm, LANE), 1)
    lo = lane < SWA_HD
    sq = k_s * k_s
    ss_lo = jnp.sum(jnp.where(lo, sq, 0.0), axis=-1, keepdims=True)
    ss_hi = jnp.sum(jnp.where(lo, 0.0, sq), axis=-1, keepdims=True)
    rstd = jnp.where(lo, lax.rsqrt(ss_lo * (1.0 / SWA_HD) + EPS), lax.rsqrt(ss_hi * (1.0 / SWA_HD) + EPS))
    y = k_s * rstd * gk_swa_ref[...]
    out = y * cs_ref[...] + pltpu.roll(y, 96, 1) * sas_ref[...] + pltpu.roll(y, 32, 1) * sbs_ref[...]
    k_swa_ref[...] = out.astype(BF16)

    gqm = gq_mem_ref[...]
    for hd in range(MEM_HEADS):
        blk = q_m[:, hd * MEM_HD:(hd + 1) * MEM_HD]
        ss = jnp.sum(blk * blk, axis=-1, keepdims=True)
        y = blk * lax.rsqrt(ss * (1.0 / MEM_HD) + EPS) * gqm
        q_mem_ref[:, hd * MEM_HD:(hd + 1) * MEM_HD] = y.astype(BF16)


def _proj(x, w, tabs, tm):
    b, s, d = x.shape
    grid = (b, s // tm)
    tok = lambda width: pl.BlockSpec((None, tm, width), lambda i, j: (i, j, 0))
    tok_t = lambda rows: pl.BlockSpec((None, rows, tm), lambda i, j: (i, 0, j))
    tab = lambda width: pl.BlockSpec((tm, width), lambda i, j: (j, 0))
    tab_t = lambda rows: pl.BlockSpec((rows, tm), lambda i, j: (0, j))
    consts = [w['g_mix'], w['wa'], w['wbt'], w['q_a_norm'], w['wqt'], w['kv_a_norm'], w['wk'], w['wvt'],
              w['gq_mla'], w['gk_mla'], w['gq_swa'], w['gk_swa'], w['gq_mem']]
    in_specs = [tok(d)] + [_const_spec(c.shape) for c in consts] + [
        tab_t(16), tab_t(16), tab(LANE), tab(LANE), tab(LANE),
        tab_t(32), tab_t(32), tab(LANE), tab(LANE), tab(LANE)]
    out_shape = (
        jax.ShapeDtypeStruct((b, MLA_HEADS * MLA_HP, s), BF16),
        jax.ShapeDtypeStruct((b, s, MLA_HEADS * MLA_HP), BF16),
        jax.ShapeDtypeStruct((b, MLA_HEADS * MLA_VROWS, s), BF16),
        jax.ShapeDtypeStruct((b, SWA_HEADS * SWA_HD, s), BF16),
        jax.ShapeDtypeStruct((b, s, SWA_KV_HEADS * SWA_HD), BF16),
        jax.ShapeDtypeStruct((b, SWA_KV_HEADS * SWA_HD, s), BF16),
        jax.ShapeDtypeStruct((b, s, MEM_HEADS * MEM_HD), BF16),
    )
    out_specs = (tok_t(MLA_HEADS * MLA_HP), tok(MLA_HEADS * MLA_HP), tok_t(MLA_HEADS * MLA_VROWS),
                 tok_t(SWA_HEADS * SWA_HD), tok(SWA_KV_HEADS * SWA_HD), tok_t(SWA_KV_HEADS * SWA_HD),
                 tok(MEM_HEADS * MEM_HD))
    return pl.pallas_call(
        _proj_kernel, grid=grid, in_specs=in_specs, out_specs=out_specs, out_shape=out_shape,
        compiler_params=_params(("parallel", "parallel")), name="proj",
    )(x, *consts, tabs['ct16'], tabs['st16'], tabs['ck'], tabs['sak'], tabs['sbk'],
      tabs['ct32'], tabs['st32'], tabs['cs'], tabs['sas'], tabs['sbs'])


def _mem_kv_kernel(mem_ref, gmem_ref, wkt_ref, wv_ref, gk_ref, kt_ref, v_ref):
    m = mem_ref[...]
    mn = (m * lax.rsqrt(jnp.mean(m * m, axis=-1, keepdims=True) + EPS) * gmem_ref[...]).astype(BF16)
    kt = _dot_nt(wkt_ref[...], mn)
    gk = gk_ref[...]
    for hd in range(MEM_HEADS):
        blk = kt[hd * MEM_HD:(hd + 1) * MEM_HD]
        ss = jnp.sum(blk * blk, axis=0, keepdims=True)
        kt_ref[hd * MEM_HD:(hd + 1) * MEM_HD, :] = (blk * lax.rsqrt(ss * (1.0 / MEM_HD) + EPS) * gk).astype(BF16)
    v_ref[...] = _dot(mn, wv_ref[...]).astype(BF16)


def _mem_kv(mem, w):
    b, n, d = mem.shape
    hw = MEM_HEADS * MEM_HD
    consts = [w['g_mem'], w['wmkt'], w['wmv'], w['gk_mem']]
    return pl.pallas_call(
        _mem_kv_kernel, grid=(b,),
        in_specs=[pl.BlockSpec((None, n, d), lambda i: (i, 0, 0))] + [_const_spec(c.shape) for c in consts],
        out_specs=(pl.BlockSpec((None, hw, n), lambda i: (i, 0, 0)), pl.BlockSpec((None, n, hw), lambda i: (i, 0, 0))),
        out_shape=(jax.ShapeDtypeStruct((b, hw, n), BF16), jax.ShapeDtypeStruct((b, n, hw), BF16)),
        compiler_params=_params(("parallel",)), name="mem_kv",
    )(mem, *consts)


def _aligned(start, align):
    return start if isinstance(start, int) else pl.multiple_of(start, align)


def _mla_kernel(qt_ref, k_ref, vt_ref, ot_ref, s0_ref, s1_ref, p0_ref, p1_ref, *, tk, sub):
    s = k_ref.shape[0]
    tq = qt_ref.shape[1]
    n = s // tk
    qt = qt_ref[...]
    s_refs = (s0_ref, s1_ref)
    p_refs = (p0_ref, p1_ref)

    def step(c, par, carry, logits=True, accum=True, exps=True):
        mc, m, alpha, acc = carry
        if exps:
            m_new = jnp.maximum(m, mc)
            alpha_new = jnp.exp2(m - m_new)
        if accum:
            acc = alpha * acc
        mc_next = None
        for j in range(tk // sub):
            blk = slice(j * sub, (j + 1) * sub)
            if logits:
                off = _aligned((c + 1) * tk + j * sub, sub)
                st = _dot(k_ref[pl.ds(off, sub), :], qt)
                s_refs[1 - par][blk, :] = st
                mj = jnp.max(st, axis=0, keepdims=True)
                mc_next = mj if mc_next is None else jnp.maximum(mc_next, mj)
            if accum:
                off = _aligned((c - 1) * tk + j * sub, sub)
                acc = acc + _dot(vt_ref[:, pl.ds(off, sub)], p_refs[1 - par][blk, :])
            if exps:
                p_refs[par][blk, :] = jnp.exp2((s_refs[par][blk, :] - m_new).astype(BF16))
        return (mc_next if logits else mc, m_new if exps else m, alpha_new if exps else alpha, acc)

    def pair(i, carry):
        return step(2 * i + 2, 0, step(2 * i + 1, 1, carry))

    init = jnp.full((1, tq), NEG_BIG, F32)
    carry = (init, init, jnp.zeros((1, tq), F32), jnp.zeros((MLA_VROWS, tq), F32))
    carry = step(-1, 1, carry, accum=False, exps=False)
    carry = step(0, 0, carry, accum=False)
    carry = lax.fori_loop(0, n // 2 - 1, pair, carry, unroll=True)
    carry = step(n - 1, 1, carry, logits=False)
    _, _, _, acc = step(n, 0, carry, logits=False, exps=False)
    ot_ref[...] = (acc[0:MLA_V] / acc[MLA_V:MLA_V + 1]).astype(BF16)


def _mla_attn(qt, k, vt, tq, tk):
    b, _, s = qt.shape
    grid = (b, MLA_HEADS, s // tq)
    assert (s // tk) % 2 == 0
    return pl.pallas_call(
        functools.partial(_mla_kernel, tk=tk, sub=min(tk, 256)), grid=grid,
        in_specs=[pl.BlockSpec((None, MLA_HP, tq), lambda i, h, j: (i, h, j)),
                  pl.BlockSpec((None, s, MLA_HP), lambda i, h, j: (i, 0, h)),
                  pl.BlockSpec((None, MLA_VROWS, s), lambda i, h, j: (i, h, 0))],
        out_specs=pl.BlockSpec((None, MLA_V, tq), lambda i, h, j: (i, h, j)),
        out_shape=jax.ShapeDtypeStruct((b, MLA_HEADS * MLA_V, s), BF16),
        scratch_shapes=[pltpu.VMEM((tk, tq), F32), pltpu.VMEM((tk, tq), F32),
                        pltpu.VMEM((tk, tq), BF16), pltpu.VMEM((tk, tq), BF16)],
        compiler_params=_params(("parallel", "parallel", "arbitrary")), name="mla_attn",
    )(qt, k, vt)


def _swa_kernel(qt_ref, k_ref, vt_ref, sink_ref, ot_ref):
    s = k_ref.shape[0]
    t = qt_ref.shape[1]
    span = 3 * BLOCK
    gw = SWA_GROUP * BLOCK
    zeros = jnp.zeros((SWA_HD, gw), BF16)
    for blk in range(t // BLOCK):
        n = pl.program_id(1) * (t // BLOCK) + blk
        start = pl.multiple_of(jnp.clip((n - 1) * BLOCK, 0, s - span), BLOCK)
        kwin = k_ref[pl.ds(start, span), :]
        vwin = vt_ref[:, pl.ds(start, span)]
        kpos = start + lax.broadcasted_iota(jnp.int32, (span, gw), 0)
        qpos = n * BLOCK + (lax.broadcasted_iota(jnp.int32, (span, gw), 1) & (BLOCK - 1))
        mask = jnp.abs(qpos - kpos) <= WINDOW
        for g in range(SWA_KV_HEADS):
            qg = jnp.concatenate(
                [qt_ref[(g * SWA_GROUP + j) * SWA_HD:(g * SWA_GROUP + j + 1) * SWA_HD, blk * BLOCK:(blk + 1) * BLOCK]
                 for j in range(SWA_GROUP)], axis=1)
            qpad = jnp.concatenate([qg, zeros] if g == 0 else [zeros, qg], axis=0)
            st = jnp.where(mask, _dot(kwin, qpad), NEG_BIG)
            sk = sink_ref[g:g + 1, :]
            m = jnp.maximum(jnp.max(st, axis=0, keepdims=True), sk)
            p = jnp.exp2(st - m)
            denom = jnp.sum(p, axis=0, keepdims=True) + jnp.exp2(sk - m)
            ot = _dot(vwin[g * SWA_HD:(g + 1) * SWA_HD], p.astype(BF16)) / denom
            for j in range(SWA_GROUP):
                hd = g * SWA_GROUP + j
                ot_ref[hd * SWA_HD:(hd + 1) * SWA_HD, blk * BLOCK:(blk + 1) * BLOCK] = (
                    ot[:, j * BLOCK:(j + 1) * BLOCK].astype(BF16))


def _swa_attn(qt, k, vt, sink_rows, t):
    b, hw, s = qt.shape
    kvw = SWA_KV_HEADS * SWA_HD
    return pl.pallas_call(
        _swa_kernel, grid=(b, s // t),
        in_specs=[pl.BlockSpec((None, hw, t), lambda i, j: (i, 0, j)),
                  pl.BlockSpec((None, s, kvw), lambda i, j: (i, 0, 0)),
                  pl.BlockSpec((None, kvw, s), lambda i, j: (i, 0, 0)),
                  _const_spec(sink_rows.shape)],
        out_specs=pl.BlockSpec((None, hw, t), lambda i, j: (i, 0, j)),
        out_shape=jax.ShapeDtypeStruct((b, hw, s), BF16),
        compiler_params=_params(("parallel", "arbitrary")), name="swa_attn",
    )(qt, k, vt, sink_rows)


def _sigmoid(v):
    return 1.0 / (1.0 + jnp.exp(-v))


def _merge_kernel(x_ref, ot_mla_ref, ot_swa_ref, q_mem_ref, kt_mem_ref, v_mem_ref, gmix_ref, wg_ref,
                  wo_mla_ref, wo_swa_ref, wo_mem_ref, wout_ref, y_ref):
    x = x_ref[...]
    hb = (x * lax.rsqrt(jnp.mean(x * x, axis=-1, keepdims=True) + EPS) * gmix_ref[...]).astype(BF16)

    o_heads = []
    for hd in range(MEM_HEADS):
        q = q_mem_ref[:, hd * MEM_HD:(hd + 1) * MEM_HD]
        sc = _dot(q, kt_mem_ref[hd * MEM_HD:(hd + 1) * MEM_HD, :])
        p = jnp.exp2(sc - jnp.max(sc, axis=-1, keepdims=True))
        l = jnp.sum(p, axis=-1, keepdims=True)
        o_heads.append(_dot(p.astype(BF16), v_mem_ref[:, hd * MEM_HD:(hd + 1) * MEM_HD]) / l)
    o_mem = jnp.concatenate(o_heads, axis=1).astype(BF16)

    merged = _sigmoid(_dot(hb, wg_ref[:, 0:D_MODEL])) * _dot_tn(ot_mla_ref[...], wo_mla_ref[...])
    merged += _sigmoid(_dot(hb, wg_ref[:, D_MODEL:2 * D_MODEL])) * _dot_tn(ot_swa_ref[...], wo_swa_ref[...])
    merged += _sigmoid(_dot(hb, wg_ref[:, 2 * D_MODEL:3 * D_MODEL])) * _dot(o_mem, wo_mem_ref[...])
    y_ref[...] = x + _dot(merged.astype(BF16), wout_ref[...])


def _merge(x, ot_mla, ot_swa, q_mem, kt_mem, v_mem, w, tm):
    b, s, d = x.shape
    n_mem = v_mem.shape[1]
    hw = MEM_HEADS * MEM_HD
    tok = lambda width: pl.BlockSpec((None, tm, width), lambda i, j: (i, j, 0))
    tok_t = lambda rows: pl.BlockSpec((None, rows, tm), lambda i, j: (i, 0, j))
    consts = [w['g_mix'], w['wg'], w['wo_mla'], w['wo_swa'], w['wo_mem'], w['wout']]
    return pl.pallas_call(
        _merge_kernel, grid=(b, s // tm),
        in_specs=[tok(d), tok_t(MLA_HEADS * MLA_V), tok_t(SWA_HEADS * SWA_HD), tok(hw),
                  pl.BlockSpec((None, hw, n_mem), lambda i, j: (i, 0, 0)),
                  pl.BlockSpec((None, n_mem, hw), lambda i, j: (i, 0, 0))] + [_const_spec(c.shape) for c in consts],
        out_specs=tok(d), out_shape=jax.ShapeDtypeStruct((b, s, d), F32),
        compiler_params=_params(("parallel", "parallel")), name="merge",
    )(x, ot_mla, ot_swa, q_mem, kt_mem, v_mem, *consts)


HALO = 8


def _ffn_kernel(x_ref, prev_ref, next_ref, gffn_ref, wup_ref, cw_ref, cb_ref, wdown_ref, y_ref):
    j = pl.program_id(1)
    nj = pl.num_programs(1)
    x = x_ref[...]
    tm = x.shape[0]
    g = gffn_ref[...]

    def norm(v):
        return v * lax.rsqrt(jnp.mean(v * v, axis=-1, keepdims=True) + EPS) * g

    hp = jnp.where(j > 0, norm(prev_ref[...]), 0.0)
    hn = jnp.where(j < nj - 1, norm(next_ref[...]), 0.0)
    hext = jnp.concatenate([hp, norm(x), hn], axis=0).astype(BF16)
    u = _dot(hext, wup_ref[...])
    cw = cw_ref[...]
    conv = (u[HALO - 1:HALO - 1 + tm] * cw[0:1] + u[HALO:HALO + tm] * cw[1:2]
            + u[HALO + 1:HALO + 1 + tm] * cw[2:3] + cb_ref[...])
    a = conv[:, :D_FF]
    val = conv[:, D_FF:]
    act = (a * _sigmoid(a) * val).astype(BF16)
    y_ref[...] = x + _dot(act, wdown_ref[...])


def _ffn(x, w, tm):
    b, s, d = x.shape
    nh = tm // HALO
    last = s // HALO - 1
    consts = [w['g_ffn'], w['wup'], w['conv_w'], w['conv_b'], w['wdown']]
    return pl.pallas_call(
        _ffn_kernel, grid=(b, s // tm),
        in_specs=[pl.BlockSpec((None, tm, d), lambda i, j: (i, j, 0)),
                  pl.BlockSpec((None, HALO, d), lambda i, j: (i, jnp.maximum(j * nh - 1, 0), 0)),
                  pl.BlockSpec((None, HALO, d), lambda i, j: (i, jnp.minimum((j + 1) * nh, last), 0))]
                 + [_const_spec(c.shape) for c in consts],
        out_specs=pl.BlockSpec((None, tm, d), lambda i, j: (i, j, 0)),
        out_shape=jax.ShapeDtypeStruct((b, s, d), F32),
        compiler_params=_params(("parallel", "parallel")), name="ffn",
    )(x, x, x, *consts)


def _prep_weights(g_mix, g_mem, w_in, q_a_norm, w_q_b, kv_a_norm, w_kv_b, g_q_mla, g_k_mla, g_q_swa, g_k_swa,
                  swa_sink, w_mem_kv, g_q_mem, g_k_mem, w_o_mla, w_o_swa, w_o_mem, w_out, g_ffn, w_up, conv_w,
                  conv_b, w_down):
    offs = [0]
    for sp in SPLITS:
        offs.append(offs[-1] + sp)
    w_cq, w_ckv, w_kr, w_qs, w_ks, w_vs, w_qm, w_gate = (w_in[:, offs[i]:offs[i + 1]] for i in range(8))
    row = lambda v: v.reshape(1, -1).astype(F32)
    col = lambda v: v.reshape(-1, 1).astype(F32)
    w = {}
    w['g_mix'] = row(g_mix)
    w['g_mem'] = row(g_mem)
    w['g_ffn'] = row(g_ffn)
    w['q_a_norm'] = row(q_a_norm)
    w['kv_a_norm'] = row(kv_a_norm)
    w['wa'] = jnp.concatenate([w_cq, w_ckv, w_ks, w_qm, jnp.pad(w_kr, ((0, 0), (0, LANE - MLA_ROPE)))], axis=1).astype(BF16)
    w['wbt'] = jnp.concatenate([w_qs, w_vs], axis=1).T.astype(BF16)
    w['wg'] = w_gate.astype(BF16)
    wq = jnp.pad(w_q_b.reshape(Q_LORA, MLA_HEADS, MLA_QK), ((0, 0), (0, 0), (0, MLA_HP - MLA_QK)))
    w['wqt'] = wq.reshape(Q_LORA, MLA_HEADS * MLA_HP).T.astype(BF16)
    wkv = w_kv_b.reshape(KV_LORA, MLA_HEADS, MLA_NOPE + MLA_V)
    wk_nope = jnp.pad(wkv[:, :, :MLA_NOPE], ((0, 0), (0, 0), (0, MLA_HP - MLA_NOPE))).reshape(KV_LORA, -1)
    place = jnp.zeros((LANE, MLA_HP), F32).at[jnp.arange(MLA_ROPE), MLA_NOPE + jnp.arange(MLA_ROPE)].set(1.0)
    w['wk'] = jnp.concatenate([wk_nope, jnp.tile(place, (1, MLA_HEADS))], axis=0).astype(BF16)
    w['wvt'] = wkv[:, :, MLA_NOPE:].reshape(KV_LORA, MLA_HEADS * MLA_V).T.astype(BF16)
    sc_mla = MLA_QK ** -0.5 * LOG2E
    w['gq_mla'] = col(jnp.pad(g_q_mla * sc_mla, (0, MLA_HP - MLA_QK)))
    w['gk_mla'] = row(jnp.pad(g_k_mla, (0, MLA_HP - MLA_QK)))
    w['gq_swa'] = col(g_q_swa * (SWA_HD ** -0.5 * LOG2E))
    w['gk_swa'] = row(jnp.tile(g_k_swa, SWA_KV_HEADS))
    w['gq_mem'] = row(g_q_mem * (MEM_HD ** -0.5 * LOG2E))
    w['gk_mem'] = col(g_k_mem)
    w['sink_rows'] = jnp.repeat(swa_sink.astype(F32) * LOG2E, BLOCK).reshape(SWA_KV_HEADS, SWA_GROUP * BLOCK)
    hw = MEM_HEADS * MEM_HD
    w['wmkt'] = w_mem_kv[:, :hw].T.astype(BF16)
    w['wmv'] = w_mem_kv[:, hw:].astype(BF16)
    w['wo_mla'] = w_o_mla.astype(BF16)
    w['wo_swa'] = w_o_swa.astype(BF16)
    w['wo_mem'] = w_o_mem.astype(BF16)
    w['wout'] = w_out.astype(BF16)
    w['wup'] = w_up.astype(BF16)
    w['wdown'] = w_down.astype(BF16)
    w['conv_w'] = conv_w.astype(F32)
    w['conv_b'] = row(conv_b)
    return w


def _rope_tables(s):
    def cs(dim):
        inv = 1.0 / (ROPE_THETA ** (jnp.arange(0, dim, 2, dtype=F32) / dim))
        ang = jnp.arange(s, dtype=F32)[:, None] * inv[None, :]
        return jnp.cos(ang), jnp.sin(ang)

    c16, s16 = cs(MLA_ROPE)
    c32, s32 = cs(SWA_HD)
    z = lambda n: jnp.zeros((s, n), F32)
    t = {'ct16': c16.T, 'st16': s16.T, 'ct32': c32.T, 'st32': s32.T}
    t['ck'] = jnp.concatenate([jnp.ones((s, MLA_NOPE), F32), c16, c16, z(MLA_HP - MLA_QK)], axis=1)
    t['sak'] = jnp.concatenate([z(MLA_NOPE), -s16, z(16), z(MLA_HP - MLA_QK)], axis=1)
    t['sbk'] = jnp.concatenate([z(MLA_NOPE), z(16), s16, z(MLA_HP - MLA_QK)], axis=1)
    t['cs'] = jnp.concatenate([c32, c32] * SWA_KV_HEADS, axis=1)
    t['sas'] = jnp.concatenate([-s32, z(32)] * SWA_KV_HEADS, axis=1)
    t['sbs'] = jnp.concatenate([z(32), s32] * SWA_KV_HEADS, axis=1)
    return t


def _tiles(s):
    pick = lambda want: want if s % want == 0 else BLOCK
    mla_k = 1024 if s % 2048 == 0 else BLOCK
    return dict(proj=pick(256), mla_q=pick(256), mla_k=mla_k, swa=pick(512), merge=pick(256), ffn=pick(256))


def _layer(x, mem, w):
    b, s, d = x.shape
    assert d == D_MODEL and s % BLOCK == 0 and s >= 3 * BLOCK
    ts = _tiles(s)
    tabs = _rope_tables(s)
    qt_mla, k_mla, vt_mla, qt_swa, k_swa, vt_swa, q_mem = _proj(x, w, tabs, ts['proj'])
    kt_mem, v_mem = _mem_kv(mem, w)
    ot_mla = _mla_attn(qt_mla, k_mla, vt_mla, ts['mla_q'], ts['mla_k'])
    ot_swa = _swa_attn(qt_swa, k_swa, vt_swa, w['sink_rows'], ts['swa'])
    x1 = _merge(x, ot_mla, ot_swa, q_mem, kt_mem, v_mem, w, ts['merge'])
    return _ffn(x1, w, ts['ffn'])


def kernel(x_prompt, x_sample, mem_prompt, mem_sample, g_mix, g_mem, w_in, q_a_norm, w_q_b, kv_a_norm, w_kv_b,
           g_q_mla, g_k_mla, g_q_swa, g_k_swa, swa_sink, w_mem_kv, g_q_mem, g_k_mem, w_o_mla, w_o_swa, w_o_mem,
           w_out, g_ffn, w_up, conv_w, conv_b, w_down):
    weights = (g_mix, g_mem, w_in, q_a_norm, w_q_b, kv_a_norm, w_kv_b, g_q_mla, g_k_mla, g_q_swa, g_k_swa,
               swa_sink, w_mem_kv, g_q_mem, g_k_mem, w_o_mla, w_o_swa, w_o_mem, w_out, g_ffn, w_up, conv_w,
               conv_b, w_down)
    depth = g_mix.shape[0]
    y_prompt, y_sample = x_prompt, x_sample
    for layer in range(depth):
        w = _prep_weights(*(p[layer] for p in weights))
        y_prompt = _layer(y_prompt, mem_prompt, w)
        y_sample = _layer(y_sample, mem_sample, w)
    return (y_prompt, y_sample)
```

```python
import functools
import math

import jax
import jax.numpy as jnp
from jax import lax
from jax.experimental import pallas as pl
from jax.experimental.pallas import tpu as pltpu

D_MODEL = 1024
N_MEM = 256
EPS = 1e-6
ROPE_THETA = 10000.0
MLA_HEADS = 8
MLA_NOPE = 64
MLA_ROPE = 32
MLA_V = 64
MLA_QK = MLA_NOPE + MLA_ROPE
Q_LORA = 384
KV_LORA = 256
SWA_HEADS = 8
SWA_KV_HEADS = 2
SWA_GROUP = SWA_HEADS // SWA_KV_HEADS
SWA_HD = 64
WINDOW = 128
BLOCK = 128
MEM_HEADS = 4
MEM_HD = 128
N_BRANCH = 3
D_FF = 2816
SPLITS = (Q_LORA, KV_LORA, MLA_ROPE, SWA_HEADS * SWA_HD, SWA_KV_HEADS * SWA_HD,
          SWA_KV_HEADS * SWA_HD, MEM_HEADS * MEM_HD, N_BRANCH * D_MODEL)

LANE = 128
MLA_HP = 128
MLA_VROWS = 80
LOG2E = math.log2(math.e)
NEG_BIG = -1e30
MLA_LOGIT_BOUND = 80.0
VMEM_LIMIT = 56 * 1024 * 1024

BF16 = jnp.bfloat16
F32 = jnp.float32


def _dot(a, b):
    return jnp.dot(a, b, preferred_element_type=F32)


def _dot_nt(a, b):
    return lax.dot_general(a, b, (((1,), (1,)), ((), ())), preferred_element_type=F32)


def _dot_tn(a, b):
    return lax.dot_general(a, b, (((0,), (0,)), ((), ())), preferred_element_type=F32)


def _const_spec(shape):
    nd = len(shape)
    return pl.BlockSpec(shape, lambda *_: (0,) * nd, pipeline_mode=pl.Buffered(1))


def _params(semantics):
    return pltpu.CompilerParams(dimension_semantics=semantics, vmem_limit_bytes=VMEM_LIMIT)


def _proj_kernel(x_ref, gmix_ref, wa_ref, wbt_ref, qan_ref, wqt_ref, kvan_ref, wk_ref, wvt_ref,
                 gq_mla_ref, gk_mla_ref, gq_swa_ref, gk_swa_ref, gq_mem_ref,
                 ct16_ref, st16_ref, ck_ref, sak_ref, sbk_ref,
                 ct32_ref, st32_ref, cs_ref, sas_ref, sbs_ref,
                 qt_mla_ref, k_mla_ref, vt_mla_ref, qt_swa_ref, k_swa_ref, vt_swa_ref, q_mem_ref):
    x = x_ref[...]
    tm = x.shape[0]
    h = x * lax.rsqrt(jnp.mean(x * x, axis=-1, keepdims=True) + EPS) * gmix_ref[...]
    hb = h.astype(BF16)
    z = _dot(hb, wa_ref[...])
    c_q = z[:, 0:384]
    c_kv = z[:, 384:640]
    k_s = z[:, 640:768]
    q_m = z[:, 768:1280]
    k_rope = z[:, 1280:1408]

    c_qn = (c_q * lax.rsqrt(jnp.mean(c_q * c_q, axis=-1, keepdims=True) + EPS) * qan_ref[...]).astype(BF16)
    qt = _dot_nt(wqt_ref[...], c_qn)
    ct = ct16_ref[...]
    st = st16_ref[...]
    gq = gq_mla_ref[...]
    for hd in range(MLA_HEADS):
        blk = qt[hd * MLA_HP:(hd + 1) * MLA_HP]
        ss = jnp.sum(blk * blk, axis=0, keepdims=True)
        y = blk * lax.rsqrt(ss * (1.0 / MLA_QK) + EPS) * gq
        x1 = y[64:80]
        x2 = y[80:96]
        out = jnp.concatenate([y[0:64], x1 * ct - x2 * st, x2 * ct + x1 * st, y[96:128]], axis=0)
        qt_mla_ref[hd * MLA_HP:(hd + 1) * MLA_HP, :] = out.astype(BF16)

    c_kvn = (c_kv * lax.rsqrt(jnp.mean(c_kv * c_kv, axis=-1, keepdims=True) + EPS) * kvan_ref[...]).astype(BF16)
    kin = jnp.concatenate([c_kvn, k_rope.astype(BF16)], axis=1)
    kpre = _dot(kin, wk_ref[...])
    ck = ck_ref[...]
    sak = sak_ref[...]
    sbk = sbk_ref[...]
    gk = gk_mla_ref[...]
    for hd in range(MLA_HEADS):
        blk = kpre[:, hd * MLA_HP:(hd + 1) * MLA_HP]
        ss = jnp.sum(blk * blk, axis=-1, keepdims=True)
        y = blk * lax.rsqrt(ss * (1.0 / MLA_QK) + EPS) * gk
        out = y * ck + pltpu.roll(y, 112, 1) * sak + pltpu.roll(y, 16, 1) * sbk
        k_mla_ref[:, hd * MLA_HP:(hd + 1) * MLA_HP] = out.astype(BF16)
    vt = _dot_nt(wvt_ref[...], c_kvn)
    ones = jnp.ones((MLA_VROWS - MLA_V, tm), BF16)
    for hd in range(MLA_HEADS):
        vt_mla_ref[hd * MLA_VROWS:hd * MLA_VROWS + MLA_V, :] = vt[hd * MLA_V:(hd + 1) * MLA_V].astype(BF16)
        vt_mla_ref[hd * MLA_VROWS + MLA_V:(hd + 1) * MLA_VROWS, :] = ones

    bt = _dot_nt(wbt_ref[...], hb)
    c32 = ct32_ref[...]
    s32 = st32_ref[...]
    gqs = gq_swa_ref[...]
    for hd in range(SWA_HEADS):
        blk = bt[hd * SWA_HD:(hd + 1) * SWA_HD]
        ss = jnp.sum(blk * blk, axis=0, keepdims=True)
        y = blk * lax.rsqrt(ss * (1.0 / SWA_HD) + EPS) * gqs
        x1 = y[0:32]
        x2 = y[32:64]
        out = jnp.concatenate([x1 * c32 - x2 * s32, x2 * c32 + x1 * s32], axis=0)
        qt_swa_ref[hd * SWA_HD:(hd + 1) * SWA_HD, :] = out.astype(BF16)
    vt_swa_ref[...] = bt[SWA_HEADS * SWA_HD:].astype(BF16)
    lane = lax.broadcasted_iota(jnp.int32, (tm, LANE), 1)
    lo = lane < SWA_HD
    sq = k_s * k_s
    ss_lo = jnp.sum(jnp.where(lo, sq, 0.0), axis=-1, keepdims=True)
    ss_hi = jnp.sum(jnp.where(lo, 0.0, sq), axis=-1, keepdims=True)
    rstd = jnp.where(lo, lax.rsqrt(ss_lo * (1.0 / SWA_HD) + EPS), lax.rsqrt(ss_hi * (1.0 / SWA_HD) + EPS))
    y = k_s * rstd * gk_swa_ref[...]
    out = y * cs_ref[...] + pltpu.roll(y, 96, 1) * sas_ref[...] + pltpu.roll(y, 32, 1) * sbs_ref[...]
    k_swa_ref[...] = out.astype(BF16)

    gqm = gq_mem_ref[...]
    for hd in range(MEM_HEADS):
        blk = q_m[:, hd * MEM_HD:(hd + 1) * MEM_HD]
        ss = jnp.sum(blk * blk, axis=-1, keepdims=True)
        y = blk * lax.rsqrt(ss * (1.0 / MEM_HD) + EPS) * gqm
        q_mem_ref[:, hd * MEM_HD:(hd + 1) * MEM_HD] = y.astype(BF16)


def _proj(x, w, tabs, tm):
    b, s, d = x.shape
    grid = (b, s // tm)
    tok = lambda width: pl.BlockSpec((None, tm, width), lambda i, j: (i, j, 0))
    tok_t = lambda rows: pl.BlockSpec((None, rows, tm), lambda i, j: (i, 0, j))
    tab = lambda width: pl.BlockSpec((tm, width), lambda i, j: (j, 0))
    tab_t = lambda rows: pl.BlockSpec((rows, tm), lambda i, j: (0, j))
    consts = [w['g_mix'], w['wa'], w['wbt'], w['q_a_norm'], w['wqt'], w['kv_a_norm'], w['wk'], w['wvt'],
              w['gq_mla'], w['gk_mla'], w['gq_swa'], w['gk_swa'], w['gq_mem']]
    in_specs = [tok(d)] + [_const_spec(c.shape) for c in consts] + [
        tab_t(16), tab_t(16), tab(LANE), tab(LANE), tab(LANE),
        tab_t(32), tab_t(32), tab(LANE), tab(LANE), tab(LANE)]
    out_shape = (
        jax.ShapeDtypeStruct((b, MLA_HEADS * MLA_HP, s), BF16),
        jax.ShapeDtypeStruct((b, s, MLA_HEADS * MLA_HP), BF16),
        jax.ShapeDtypeStruct((b, MLA_HEADS * MLA_VROWS, s), BF16),
        jax.ShapeDtypeStruct((b, SWA_HEADS * SWA_HD, s), BF16),
        jax.ShapeDtypeStruct((b, s, SWA_KV_HEADS * SWA_HD), BF16),
        jax.ShapeDtypeStruct((b, SWA_KV_HEADS * SWA_HD, s), BF16),
        jax.ShapeDtypeStruct((b, s, MEM_HEADS * MEM_HD), BF16),
    )
    out_specs = (tok_t(MLA_HEADS * MLA_HP), tok(MLA_HEADS * MLA_HP), tok_t(MLA_HEADS * MLA_VROWS),
                 tok_t(SWA_HEADS * SWA_HD), tok(SWA_KV_HEADS * SWA_HD), tok_t(SWA_KV_HEADS * SWA_HD),
                 tok(MEM_HEADS * MEM_HD))
    return pl.pallas_call(
        _proj_kernel, grid=grid, in_specs=in_specs, out_specs=out_specs, out_shape=out_shape,
        compiler_params=_params(("parallel", "parallel")), name="proj",
    )(x, *consts, tabs['ct16'], tabs['st16'], tabs['ck'], tabs['sak'], tabs['sbk'],
      tabs['ct32'], tabs['st32'], tabs['cs'], tabs['sas'], tabs['sbs'])


def _mem_kv_kernel(mem_ref, gmem_ref, wkt_ref, wv_ref, gk_ref, kt_ref, v_ref):
    m = mem_ref[...]
    mn = (m * lax.rsqrt(jnp.mean(m * m, axis=-1, keepdims=True) + EPS) * gmem_ref[...]).astype(BF16)
    kt = _dot_nt(wkt_ref[...], mn)
    gk = gk_ref[...]
    for hd in range(MEM_HEADS):
        blk = kt[hd * MEM_HD:(hd + 1) * MEM_HD]
        ss = jnp.sum(blk * blk, axis=0, keepdims=True)
        kt_ref[hd * MEM_HD:(hd + 1) * MEM_HD, :] = (blk * lax.rsqrt(ss * (1.0 / MEM_HD) + EPS) * gk).astype(BF16)
    v_ref[...] = _dot(mn, wv_ref[...]).astype(BF16)


def _mem_kv(mem, w):
    b, n, d = mem.shape
    hw = MEM_HEADS * MEM_HD
    consts = [w['g_mem'], w['wmkt'], w['wmv'], w['gk_mem']]
    return pl.pallas_call(
        _mem_kv_kernel, grid=(b,),
        in_specs=[pl.BlockSpec((None, n, d), lambda i: (i, 0, 0))] + [_const_spec(c.shape) for c in consts],
        out_specs=(pl.BlockSpec((None, hw, n), lambda i: (i, 0, 0)), pl.BlockSpec((None, n, hw), lambda i: (i, 0, 0))),
        out_shape=(jax.ShapeDtypeStruct((b, hw, n), BF16), jax.ShapeDtypeStruct((b, n, hw), BF16)),
        compiler_params=_params(("parallel",)), name="mem_kv",
    )(mem, *consts)


def _aligned(start, align):
    return start if isinstance(start, int) else pl.multiple_of(start, align)


def _mla_kernel(qt_ref, k_ref, vt_ref, ot_ref, s0_ref, s1_ref, p0_ref, p1_ref, *, tk, sub):
    s = k_ref.shape[0]
    tq = qt_ref.shape[1]
    n = s // tk
    qt = qt_ref[...]
    s_refs = (s0_ref, s1_ref)
    p_refs = (p0_ref, p1_ref)

    def step(c, par, carry, logits=True, accum=True, exps=True):
        mc, m, alpha, acc = carry
        if exps:
            m_new = jnp.maximum(m, mc)
            alpha_new = jnp.exp2(m - m_new)
        if accum:
            acc = alpha * acc
        mc_next = None
        for j in range(tk // sub):
            blk = slice(j * sub, (j + 1) * sub)
            if logits:
                off = _aligned((c + 1) * tk + j * sub, sub)
                st = _dot(k_ref[pl.ds(off, sub), :], qt)
                s_refs[1 - par][blk, :] = st
                mj = jnp.max(st, axis=0, keepdims=True)
                mc_next = mj if mc_next is None else jnp.maximum(mc_next, mj)
            if accum:
                off = _aligned((c - 1) * tk + j * sub, sub)
                acc = acc + _dot(vt_ref[:, pl.ds(off, sub)], p_refs[1 - par][blk, :])
            if exps:
                p_refs[par][blk, :] = jnp.exp2((s_refs[par][blk, :] - m_new).astype(BF16))
        return (mc_next if logits else mc, m_new if exps else m, alpha_new if exps else alpha, acc)

    def pair(i, carry):
        return step(2 * i + 2, 0, step(2 * i + 1, 1, carry))

    init = jnp.full((1, tq), NEG_BIG, F32)
    carry = (init, init, jnp.zeros((1, tq), F32), jnp.zeros((MLA_VROWS, tq), F32))
    carry = step(-1, 1, carry, accum=False, exps=False)
    carry = step(0, 0, carry, accum=False)
    carry = lax.fori_loop(0, n // 2 - 1, pair, carry, unroll=True)
    carry = step(n - 1, 1, carry, logits=False)
    _, _, _, acc = step(n, 0, carry, logits=False, exps=False)
    ot_ref[...] = (acc[0:MLA_V] / acc[MLA_V:MLA_V + 1]).astype(BF16)


def _mla_bounded_kernel(qt_ref, k_ref, vt_ref, ot_ref, *, sub):
    s = k_ref.shape[0]
    tq = qt_ref.shape[1]
    qt = qt_ref[...]
    acc = jnp.zeros((MLA_VROWS, tq), F32)
    for j in range(s // sub):
        blk = slice(j * sub, (j + 1) * sub)
        pt = jnp.exp2(_dot(k_ref[blk, :], qt)).astype(BF16)
        acc = acc + _dot(vt_ref[:, blk], pt)
    ot_ref[...] = (acc[0:MLA_V] / acc[MLA_V:MLA_V + 1]).astype(BF16)


def _mla_attn(qt, k, vt, tq, tk, bounded):
    b, _, s = qt.shape
    grid = (b, MLA_HEADS, s // tq)
    assert (s // tk) % 2 == 0
    if bounded:
        body = functools.partial(_mla_bounded_kernel, sub=min(s, 2048))
        scratch = []
    else:
        body = functools.partial(_mla_kernel, tk=tk, sub=min(tk, 256))
        scratch = [pltpu.VMEM((tk, tq), F32), pltpu.VMEM((tk, tq), F32),
                   pltpu.VMEM((tk, tq), BF16), pltpu.VMEM((tk, tq), BF16)]
    return pl.pallas_call(
        body, grid=grid,
        in_specs=[pl.BlockSpec((None, MLA_HP, tq), lambda i, h, j: (i, h, j)),
                  pl.BlockSpec((None, s, MLA_HP), lambda i, h, j: (i, 0, h)),
                  pl.BlockSpec((None, MLA_VROWS, s), lambda i, h, j: (i, h, 0))],
        out_specs=pl.BlockSpec((None, MLA_V, tq), lambda i, h, j: (i, h, j)),
        out_shape=jax.ShapeDtypeStruct((b, MLA_HEADS * MLA_V, s), BF16),
        scratch_shapes=scratch,
        compiler_params=_params(("parallel", "parallel", "arbitrary")),
        name="mla_attn_bounded" if bounded else "mla_attn",
    )(qt, k, vt)


def _swa_kernel(qt_ref, k_ref, vt_ref, sink_ref, ot_ref):
    s = k_ref.shape[0]
    t = qt_ref.shape[1]
    span = 3 * BLOCK
    gw = SWA_GROUP * BLOCK
    zeros = jnp.zeros((SWA_HD, gw), BF16)
    for blk in range(t // BLOCK):
        n = pl.program_id(1) * (t // BLOCK) + blk
        start = pl.multiple_of(jnp.clip((n - 1) * BLOCK, 0, s - span), BLOCK)
        kwin = k_ref[pl.ds(start, span), :]
        vwin = vt_ref[:, pl.ds(start, span)]
        kpos = start + lax.broadcasted_iota(jnp.int32, (span, gw), 0)
        qpos = n * BLOCK + (lax.broadcasted_iota(jnp.int32, (span, gw), 1) & (BLOCK - 1))
        mask = jnp.abs(qpos - kpos) <= WINDOW
        for g in range(SWA_KV_HEADS):
            qg = jnp.concatenate(
                [qt_ref[(g * SWA_GROUP + j) * SWA_HD:(g * SWA_GROUP + j + 1) * SWA_HD, blk * BLOCK:(blk + 1) * BLOCK]
                 for j in range(SWA_GROUP)], axis=1)
            qpad = jnp.concatenate([qg, zeros] if g == 0 else [zeros, qg], axis=0)
            st = jnp.where(mask, _dot(kwin, qpad), NEG_BIG)
            sk = sink_ref[g:g + 1, :]
            m = jnp.maximum(jnp.max(st, axis=0, keepdims=True), sk)
            p = jnp.exp2(st - m)
            denom = jnp.sum(p, axis=0, keepdims=True) + jnp.exp2(sk - m)
            ot = _dot(vwin[g * SWA_HD:(g + 1) * SWA_HD], p.astype(BF16)) / denom
            for j in range(SWA_GROUP):
                hd = g * SWA_GROUP + j
                ot_ref[hd * SWA_HD:(hd + 1) * SWA_HD, blk * BLOCK:(blk + 1) * BLOCK] = (
                    ot[:, j * BLOCK:(j + 1) * BLOCK].astype(BF16))


def _swa_attn(qt, k, vt, sink_rows, t):
    b, hw, s = qt.shape
    kvw = SWA_KV_HEADS * SWA_HD
    return pl.pallas_call(
        _swa_kernel, grid=(b, s // t),
        in_specs=[pl.BlockSpec((None, hw, t), lambda i, j: (i, 0, j)),
                  pl.BlockSpec((None, s, kvw), lambda i, j: (i, 0, 0)),
                  pl.BlockSpec((None, kvw, s), lambda i, j: (i, 0, 0)),
                  _const_spec(sink_rows.shape)],
        out_specs=pl.BlockSpec((None, hw, t), lambda i, j: (i, 0, j)),
        out_shape=jax.ShapeDtypeStruct((b, hw, s), BF16),
        compiler_params=_params(("parallel", "arbitrary")), name="swa_attn",
    )(qt, k, vt, sink_rows)


def _sigmoid(v):
    return 1.0 / (1.0 + jnp.exp(-v))


def _merge_kernel(x_ref, ot_mla_ref, ot_swa_ref, q_mem_ref, kt_mem_ref, v_mem_ref, gmix_ref, wg_ref,
                  wo_mla_ref, wo_swa_ref, wo_mem_ref, wout_ref, y_ref):
    x = x_ref[...]
    hb = (x * lax.rsqrt(jnp.mean(x * x, axis=-1, keepdims=True) + EPS) * gmix_ref[...]).astype(BF16)

    o_heads = []
    for hd in range(MEM_HEADS):
        q = q_mem_ref[:, hd * MEM_HD:(hd + 1) * MEM_HD]
        sc = _dot(q, kt_mem_ref[hd * MEM_HD:(hd + 1) * MEM_HD, :])
        p = jnp.exp2(sc - jnp.max(sc, axis=-1, keepdims=True))
        l = jnp.sum(p, axis=-1, keepdims=True)
        o_heads.append(_dot(p.astype(BF16), v_mem_ref[:, hd * MEM_HD:(hd + 1) * MEM_HD]) / l)
    o_mem = jnp.concatenate(o_heads, axis=1).astype(BF16)

    merged = _sigmoid(_dot(hb, wg_ref[:, 0:D_MODEL])) * _dot_tn(ot_mla_ref[...], wo_mla_ref[...])
    merged += _sigmoid(_dot(hb, wg_ref[:, D_MODEL:2 * D_MODEL])) * _dot_tn(ot_swa_ref[...], wo_swa_ref[...])
    merged += _sigmoid(_dot(hb, wg_ref[:, 2 * D_MODEL:3 * D_MODEL])) * _dot(o_mem, wo_mem_ref[...])
    y_ref[...] = x + _dot(merged.astype(BF16), wout_ref[...])


def _merge(x, ot_mla, ot_swa, q_mem, kt_mem, v_mem, w, tm):
    b, s, d = x.shape
    n_mem = v_mem.shape[1]
    hw = MEM_HEADS * MEM_HD
    tok = lambda width: pl.BlockSpec((None, tm, width), lambda i, j: (i, j, 0))
    tok_t = lambda rows: pl.BlockSpec((None, rows, tm), lambda i, j: (i, 0, j))
    consts = [w['g_mix'], w['wg'], w['wo_mla'], w['wo_swa'], w['wo_mem'], w['wout']]
    return pl.pallas_call(
        _merge_kernel, grid=(b, s // tm),
        in_specs=[tok(d), tok_t(MLA_HEADS * MLA_V), tok_t(SWA_HEADS * SWA_HD), tok(hw),
                  pl.BlockSpec((None, hw, n_mem), lambda i, j: (i, 0, 0)),
                  pl.BlockSpec((None, n_mem, hw), lambda i, j: (i, 0, 0))] + [_const_spec(c.shape) for c in consts],
        out_specs=tok(d), out_shape=jax.ShapeDtypeStruct((b, s, d), F32),
        compiler_params=_params(("parallel", "parallel")), name="merge",
    )(x, ot_mla, ot_swa, q_mem, kt_mem, v_mem, *consts)


HALO = 8


def _ffn_kernel(x_ref, prev_ref, next_ref, gffn_ref, wup_ref, cw_ref, cb_ref, wdown_ref, y_ref):
    j = pl.program_id(1)
    nj = pl.num_programs(1)
    x = x_ref[...]
    tm = x.shape[0]
    g = gffn_ref[...]

    def norm(v):
        return v * lax.rsqrt(jnp.mean(v * v, axis=-1, keepdims=True) + EPS) * g

    hp = jnp.where(j > 0, norm(prev_ref[...]), 0.0)
    hn = jnp.where(j < nj - 1, norm(next_ref[...]), 0.0)
    hext = jnp.concatenate([hp, norm(x), hn], axis=0).astype(BF16)
    u = _dot(hext, wup_ref[...])
    cw = cw_ref[...]
    conv = (u[HALO - 1:HALO - 1 + tm] * cw[0:1] + u[HALO:HALO + tm] * cw[1:2]
            + u[HALO + 1:HALO + 1 + tm] * cw[2:3] + cb_ref[...])
    a = conv[:, :D_FF]
    val = conv[:, D_FF:]
    act = (a * _sigmoid(a) * val).astype(BF16)
    y_ref[...] = x + _dot(act, wdown_ref[...])


def _ffn(x, w, tm):
    b, s, d = x.shape
    nh = tm // HALO
    last = s // HALO - 1
    consts = [w['g_ffn'], w['wup'], w['conv_w'], w['conv_b'], w['wdown']]
    return pl.pallas_call(
        _ffn_kernel, grid=(b, s // tm),
        in_specs=[pl.BlockSpec((None, tm, d), lambda i, j: (i, j, 0)),
                  pl.BlockSpec((None, HALO, d), lambda i, j: (i, jnp.maximum(j * nh - 1, 0), 0)),
                  pl.BlockSpec((None, HALO, d), lambda i, j: (i, jnp.minimum((j + 1) * nh, last), 0))]
                 + [_const_spec(c.shape) for c in consts],
        out_specs=pl.BlockSpec((None, tm, d), lambda i, j: (i, j, 0)),
        out_shape=jax.ShapeDtypeStruct((b, s, d), F32),
        compiler_params=_params(("parallel", "parallel")), name="ffn",
    )(x, x, x, *consts)


def _prep_weights(g_mix, g_mem, w_in, q_a_norm, w_q_b, kv_a_norm, w_kv_b, g_q_mla, g_k_mla, g_q_swa, g_k_swa,
                  swa_sink, w_mem_kv, g_q_mem, g_k_mem, w_o_mla, w_o_swa, w_o_mem, w_out, g_ffn, w_up, conv_w,
                  conv_b, w_down):
    offs = [0]
    for sp in SPLITS:
        offs.append(offs[-1] + sp)
    w_cq, w_ckv, w_kr, w_qs, w_ks, w_vs, w_qm, w_gate = (w_in[:, offs[i]:offs[i + 1]] for i in range(8))
    row = lambda v: v.reshape(1, -1).astype(F32)
    col = lambda v: v.reshape(-1, 1).astype(F32)
    w = {}
    w['g_mix'] = row(g_mix)
    w['g_mem'] = row(g_mem)
    w['g_ffn'] = row(g_ffn)
    w['q_a_norm'] = row(q_a_norm)
    w['kv_a_norm'] = row(kv_a_norm)
    w['wa'] = jnp.concatenate([w_cq, w_ckv, w_ks, w_qm, jnp.pad(w_kr, ((0, 0), (0, LANE - MLA_ROPE)))], axis=1).astype(BF16)
    w['wbt'] = jnp.concatenate([w_qs, w_vs], axis=1).T.astype(BF16)
    w['wg'] = w_gate.astype(BF16)
    wq = jnp.pad(w_q_b.reshape(Q_LORA, MLA_HEADS, MLA_QK), ((0, 0), (0, 0), (0, MLA_HP - MLA_QK)))
    w['wqt'] = wq.reshape(Q_LORA, MLA_HEADS * MLA_HP).T.astype(BF16)
    wkv = w_kv_b.reshape(KV_LORA, MLA_HEADS, MLA_NOPE + MLA_V)
    wk_nope = jnp.pad(wkv[:, :, :MLA_NOPE], ((0, 0), (0, 0), (0, MLA_HP - MLA_NOPE))).reshape(KV_LORA, -1)
    place = jnp.zeros((LANE, MLA_HP), F32).at[jnp.arange(MLA_ROPE), MLA_NOPE + jnp.arange(MLA_ROPE)].set(1.0)
    w['wk'] = jnp.concatenate([wk_nope, jnp.tile(place, (1, MLA_HEADS))], axis=0).astype(BF16)
    w['wvt'] = wkv[:, :, MLA_NOPE:].reshape(KV_LORA, MLA_HEADS * MLA_V).T.astype(BF16)
    sc_mla = MLA_QK ** -0.5 * LOG2E
    w['gq_mla'] = col(jnp.pad(g_q_mla * sc_mla, (0, MLA_HP - MLA_QK)))
    w['gk_mla'] = row(jnp.pad(g_k_mla, (0, MLA_HP - MLA_QK)))
    w['mla_logit_bound'] = 1.02 * MLA_QK * jnp.max(jnp.abs(w['gq_mla'])) * jnp.max(jnp.abs(w['gk_mla']))
    w['gq_swa'] = col(g_q_swa * (SWA_HD ** -0.5 * LOG2E))
    w['gk_swa'] = row(jnp.tile(g_k_swa, SWA_KV_HEADS))
    w['gq_mem'] = row(g_q_mem * (MEM_HD ** -0.5 * LOG2E))
    w['gk_mem'] = col(g_k_mem)
    w['sink_rows'] = jnp.repeat(swa_sink.astype(F32) * LOG2E, BLOCK).reshape(SWA_KV_HEADS, SWA_GROUP * BLOCK)
    hw = MEM_HEADS * MEM_HD
    w['wmkt'] = w_mem_kv[:, :hw].T.astype(BF16)
    w['wmv'] = w_mem_kv[:, hw:].astype(BF16)
    w['wo_mla'] = w_o_mla.astype(BF16)
    w['wo_swa'] = w_o_swa.astype(BF16)
    w['wo_mem'] = w_o_mem.astype(BF16)
    w['wout'] = w_out.astype(BF16)
    w['wup'] = w_up.astype(BF16)
    w['wdown'] = w_down.astype(BF16)
    w['conv_w'] = conv_w.astype(F32)
    w['conv_b'] = row(conv_b)
    return w


def _rope_tables(s):
    def cs(dim):
        inv = 1.0 / (ROPE_THETA ** (jnp.arange(0, dim, 2, dtype=F32) / dim))
        ang = jnp.arange(s, dtype=F32)[:, None] * inv[None, :]
        return jnp.cos(ang), jnp.sin(ang)

    c16, s16 = cs(MLA_ROPE)
    c32, s32 = cs(SWA_HD)
    z = lambda n: jnp.zeros((s, n), F32)
    t = {'ct16': c16.T, 'st16': s16.T, 'ct32': c32.T, 'st32': s32.T}
    t['ck'] = jnp.concatenate([jnp.ones((s, MLA_NOPE), F32), c16, c16, z(MLA_HP - MLA_QK)], axis=1)
    t['sak'] = jnp.concatenate([z(MLA_NOPE), -s16, z(16), z(MLA_HP - MLA_QK)], axis=1)
    t['sbk'] = jnp.concatenate([z(MLA_NOPE), z(16), s16, z(MLA_HP - MLA_QK)], axis=1)
    t['cs'] = jnp.concatenate([c32, c32] * SWA_KV_HEADS, axis=1)
    t['sas'] = jnp.concatenate([-s32, z(32)] * SWA_KV_HEADS, axis=1)
    t['sbs'] = jnp.concatenate([z(32), s32] * SWA_KV_HEADS, axis=1)
    return t


def _tiles(s):
    pick = lambda want: want if s % want == 0 else BLOCK
    mla_k = 1024 if s % 2048 == 0 else BLOCK
    return dict(proj=pick(256), mla_q=pick(256), mla_k=mla_k, swa=pick(512), merge=pick(512), ffn=pick(512))


def _layer(x, mem, w):
    b, s, d = x.shape
    assert d == D_MODEL and s % BLOCK == 0 and s >= 3 * BLOCK
    ts = _tiles(s)
    tabs = _rope_tables(s)
    qt_mla, k_mla, vt_mla, qt_swa, k_swa, vt_swa, q_mem = _proj(x, w, tabs, ts['proj'])
    kt_mem, v_mem = _mem_kv(mem, w)
    ot_mla = lax.cond(
        w['mla_logit_bound'] <= MLA_LOGIT_BOUND,
        lambda: _mla_attn(qt_mla, k_mla, vt_mla, ts['mla_q'], ts['mla_k'], True),
        lambda: _mla_attn(qt_mla, k_mla, vt_mla, ts['mla_q'], ts['mla_k'], False))
    ot_swa = _swa_attn(qt_swa, k_swa, vt_swa, w['sink_rows'], ts['swa'])
    x1 = _merge(x, ot_mla, ot_swa, q_mem, kt_mem, v_mem, w, ts['merge'])
    return _ffn(x1, w, ts['ffn'])


def kernel(x_prompt, x_sample, mem_prompt, mem_sample, g_mix, g_mem, w_in, q_a_norm, w_q_b, kv_a_norm, w_kv_b,
           g_q_mla, g_k_mla, g_q_swa, g_k_swa, swa_sink, w_mem_kv, g_q_mem, g_k_mem, w_o_mla, w_o_swa, w_o_mem,
           w_out, g_ffn, w_up, conv_w, conv_b, w_down):
    weights = (g_mix, g_mem, w_in, q_a_norm, w_q_b, kv_a_norm, w_kv_b, g_q_mla, g_k_mla, g_q_swa, g_k_swa,
               swa_sink, w_mem_kv, g_q_mem, g_k_mem, w_o_mla, w_o_swa, w_o_mem, w_out, g_ffn, w_up, conv_w,
               conv_b, w_down)
    depth = g_mix.shape[0]
    y_prompt, y_sample = x_prompt, x_sample
    for layer in range(depth):
        w = _prep_weights(*(p[layer] for p in weights))
        y_prompt = _layer(y_prompt, mem_prompt, w)
        y_sample = _layer(y_sample, mem_sample, w)
    return (y_prompt, y_sample)
```

```python
import functools
import math

import jax
import jax.numpy as jnp
from jax import lax
from jax.experimental import pallas as pl
from jax.experimental.pallas import tpu as pltpu

D_MODEL = 1024
N_MEM = 256
EPS = 1e-6
ROPE_THETA = 10000.0
MLA_HEADS = 8
MLA_NOPE = 64
MLA_ROPE = 32
MLA_V = 64
MLA_QK = MLA_NOPE + MLA_ROPE
Q_LORA = 384
KV_LORA = 256
SWA_HEADS = 8
SWA_KV_HEADS = 2
SWA_GROUP = SWA_HEADS // SWA_KV_HEADS
SWA_HD = 64
WINDOW = 128
BLOCK = 128
MEM_HEADS = 4
MEM_HD = 128
N_BRANCH = 3
D_FF = 2816
SPLITS = (Q_LORA, KV_LORA, MLA_ROPE, SWA_HEADS * SWA_HD, SWA_KV_HEADS * SWA_HD,
          SWA_KV_HEADS * SWA_HD, MEM_HEADS * MEM_HD, N_BRANCH * D_MODEL)

LANE = 128
MLA_HP = 128
MLA_VROWS = 80
LOG2E = math.log2(math.e)
NEG_BIG = -1e30
MLA_LOGIT_BOUND = 80.0
VMEM_LIMIT = 56 * 1024 * 1024

BF16 = jnp.bfloat16
F32 = jnp.float32


def _dot(a, b):
    return jnp.dot(a, b, preferred_element_type=F32)


def _dot_nt(a, b):
    return lax.dot_general(a, b, (((1,), (1,)), ((), ())), preferred_element_type=F32)


def _dot_tn(a, b):
    return lax.dot_general(a, b, (((0,), (0,)), ((), ())), preferred_element_type=F32)


def _const_spec(shape):
    nd = len(shape)
    return pl.BlockSpec(shape, lambda *_: (0,) * nd, pipeline_mode=pl.Buffered(1))


def _params(semantics):
    return pltpu.CompilerParams(dimension_semantics=semantics, vmem_limit_bytes=VMEM_LIMIT)


def _proj_kernel(x_ref, gmix_ref, wa_ref, wbt_ref, qan_ref, wqt_ref, kvan_ref, wk_ref, wvt_ref,
                 gq_mla_ref, gk_mla_ref, gq_swa_ref, gk_swa_ref, gq_mem_ref,
                 ct16_ref, st16_ref, ck_ref, sak_ref, sbk_ref,
                 ct32_ref, st32_ref, cs_ref, sas_ref, sbs_ref,
                 qt_mla_ref, k_mla_ref, vt_mla_ref, qt_swa_ref, k_swa_ref, vt_swa_ref, q_mem_ref):
    x = x_ref[...]
    tm = x.shape[0]
    h = x * lax.rsqrt(jnp.mean(x * x, axis=-1, keepdims=True) + EPS) * gmix_ref[...]
    hb = h.astype(BF16)
    z = _dot(hb, wa_ref[...])
    c_q = z[:, 0:384]
    c_kv = z[:, 384:640]
    k_s = z[:, 640:768]
    q_m = z[:, 768:1280]
    k_rope = z[:, 1280:1408]

    bt = _dot_nt(wbt_ref[...], hb)
    c_qn = (c_q * lax.rsqrt(jnp.mean(c_q * c_q, axis=-1, keepdims=True) + EPS) * qan_ref[...]).astype(BF16)
    c_kvn = (c_kv * lax.rsqrt(jnp.mean(c_kv * c_kv, axis=-1, keepdims=True) + EPS) * kvan_ref[...]).astype(BF16)
    qt = _dot_nt(wqt_ref[...], c_qn)
    kpre = _dot(c_kvn, wk_ref[...])
    vt = _dot_nt(wvt_ref[...], c_kvn)

    ct = ct16_ref[...]
    st = st16_ref[...]
    gq = gq_mla_ref[...]
    for hd in range(MLA_HEADS):
        blk = qt[hd * MLA_HP:(hd + 1) * MLA_HP]
        ss = jnp.sum(blk * blk, axis=0, keepdims=True)
        y = blk * lax.rsqrt(ss * (1.0 / MLA_QK) + EPS) * gq
        x1 = y[64:80]
        x2 = y[80:96]
        out = jnp.concatenate([y[0:64], x1 * ct - x2 * st, x2 * ct + x1 * st, y[96:128]], axis=0)
        qt_mla_ref[hd * MLA_HP:(hd + 1) * MLA_HP, :] = out.astype(BF16)

    gk = gk_mla_ref[...]
    ss_rope = jnp.sum(k_rope * k_rope, axis=-1, keepdims=True)
    yr = k_rope * gk
    k_roped = yr * ck_ref[...] + pltpu.roll(yr, 112, 1) * sak_ref[...] + pltpu.roll(yr, 16, 1) * sbk_ref[...]
    for hd in range(MLA_HEADS):
        blk = kpre[:, hd * MLA_HP:(hd + 1) * MLA_HP]
        ss = jnp.sum(blk * blk, axis=-1, keepdims=True) + ss_rope
        out = (blk * gk + k_roped) * lax.rsqrt(ss * (1.0 / MLA_QK) + EPS)
        k_mla_ref[:, hd * MLA_HP:(hd + 1) * MLA_HP] = out.astype(BF16)
    ones = jnp.ones((MLA_VROWS - MLA_V, tm), BF16)
    for hd in range(MLA_HEADS):
        vt_mla_ref[hd * MLA_VROWS:hd * MLA_VROWS + MLA_V, :] = vt[hd * MLA_V:(hd + 1) * MLA_V].astype(BF16)
        vt_mla_ref[hd * MLA_VROWS + MLA_V:(hd + 1) * MLA_VROWS, :] = ones

    c32 = ct32_ref[...]
    s32 = st32_ref[...]
    gqs = gq_swa_ref[...]
    for hd in range(SWA_HEADS):
        blk = bt[hd * SWA_HD:(hd + 1) * SWA_HD]
        ss = jnp.sum(blk * blk, axis=0, keepdims=True)
        y = blk * lax.rsqrt(ss * (1.0 / SWA_HD) + EPS) * gqs
        x1 = y[0:32]
        x2 = y[32:64]
        out = jnp.concatenate([x1 * c32 - x2 * s32, x2 * c32 + x1 * s32], axis=0)
        qt_swa_ref[hd * SWA_HD:(hd + 1) * SWA_HD, :] = out.astype(BF16)
    vt_swa_ref[...] = bt[SWA_HEADS * SWA_HD:].astype(BF16)
    lane = lax.broadcasted_iota(jnp.int32, (tm, LANE), 1)
    lo = lane < SWA_HD
    sq = k_s * k_s
    ss_lo = jnp.sum(jnp.where(lo, sq, 0.0), axis=-1, keepdims=True)
    ss_hi = jnp.sum(jnp.where(lo, 0.0, sq), axis=-1, keepdims=True)
    rstd = jnp.where(lo, lax.rsqrt(ss_lo * (1.0 / SWA_HD) + EPS), lax.rsqrt(ss_hi * (1.0 / SWA_HD) + EPS))
    y = k_s * rstd * gk_swa_ref[...]
    out = y * cs_ref[...] + pltpu.roll(y, 96, 1) * sas_ref[...] + pltpu.roll(y, 32, 1) * sbs_ref[...]
    k_swa_ref[...] = out.astype(BF16)

    gqm = gq_mem_ref[...]
    for hd in range(MEM_HEADS):
        blk = q_m[:, hd * MEM_HD:(hd + 1) * MEM_HD]
        ss = jnp.sum(blk * blk, axis=-1, keepdims=True)
        y = blk * lax.rsqrt(ss * (1.0 / MEM_HD) + EPS) * gqm
        q_mem_ref[:, hd * MEM_HD:(hd + 1) * MEM_HD] = y.astype(BF16)


def _proj(x, w, tabs, tm):
    b, s, d = x.shape
    grid = (s // tm, b)
    tok = lambda width: pl.BlockSpec((None, tm, width), lambda j, i: (i, j, 0))
    tok_t = lambda rows: pl.BlockSpec((None, rows, tm), lambda j, i: (i, 0, j))
    tab = lambda width: pl.BlockSpec((tm, width), lambda j, i: (j, 0))
    tab_t = lambda rows: pl.BlockSpec((rows, tm), lambda j, i: (0, j))
    consts = [w['g_mix'], w['wa'], w['wbt'], w['q_a_norm'], w['wqt'], w['kv_a_norm'], w['wk'], w['wvt'],
              w['gq_mla'], w['gk_mla'], w['gq_swa'], w['gk_swa'], w['gq_mem']]
    in_specs = [tok(d)] + [_const_spec(c.shape) for c in consts] + [
        tab_t(16), tab_t(16), tab(LANE), tab(LANE), tab(LANE),
        tab_t(32), tab_t(32), tab(LANE), tab(LANE), tab(LANE)]
    out_shape = (
        jax.ShapeDtypeStruct((b, MLA_HEADS * MLA_HP, s), BF16),
        jax.ShapeDtypeStruct((b, s, MLA_HEADS * MLA_HP), BF16),
        jax.ShapeDtypeStruct((b, MLA_HEADS * MLA_VROWS, s), BF16),
        jax.ShapeDtypeStruct((b, SWA_HEADS * SWA_HD, s), BF16),
        jax.ShapeDtypeStruct((b, s, SWA_KV_HEADS * SWA_HD), BF16),
        jax.ShapeDtypeStruct((b, SWA_KV_HEADS * SWA_HD, s), BF16),
        jax.ShapeDtypeStruct((b, s, MEM_HEADS * MEM_HD), BF16),
    )
    out_specs = (tok_t(MLA_HEADS * MLA_HP), tok(MLA_HEADS * MLA_HP), tok_t(MLA_HEADS * MLA_VROWS),
                 tok_t(SWA_HEADS * SWA_HD), tok(SWA_KV_HEADS * SWA_HD), tok_t(SWA_KV_HEADS * SWA_HD),
                 tok(MEM_HEADS * MEM_HD))
    return pl.pallas_call(
        _proj_kernel, grid=grid, in_specs=in_specs, out_specs=out_specs, out_shape=out_shape,
        compiler_params=_params(("parallel", "parallel")), name="proj",
    )(x, *consts, tabs['ct16'], tabs['st16'], tabs['ck'], tabs['sak'], tabs['sbk'],
      tabs['ct32'], tabs['st32'], tabs['cs'], tabs['sas'], tabs['sbs'])


def _mem_kv_kernel(mem_ref, gmem_ref, wkt_ref, wv_ref, gk_ref, kt_ref, v_ref):
    m = mem_ref[...]
    mn = (m * lax.rsqrt(jnp.mean(m * m, axis=-1, keepdims=True) + EPS) * gmem_ref[...]).astype(BF16)
    kt = _dot_nt(wkt_ref[...], mn)
    gk = gk_ref[...]
    for hd in range(MEM_HEADS):
        blk = kt[hd * MEM_HD:(hd + 1) * MEM_HD]
        ss = jnp.sum(blk * blk, axis=0, keepdims=True)
        kt_ref[hd * MEM_HD:(hd + 1) * MEM_HD, :] = (blk * lax.rsqrt(ss * (1.0 / MEM_HD) + EPS) * gk).astype(BF16)
    v_ref[...] = _dot(mn, wv_ref[...]).astype(BF16)


def _mem_kv(mem, w):
    b, n, d = mem.shape
    hw = MEM_HEADS * MEM_HD
    consts = [w['g_mem'], w['wmkt'], w['wmv'], w['gk_mem']]
    return pl.pallas_call(
        _mem_kv_kernel, grid=(b,),
        in_specs=[pl.BlockSpec((None, n, d), lambda i: (i, 0, 0))] + [_const_spec(c.shape) for c in consts],
        out_specs=(pl.BlockSpec((None, hw, n), lambda i: (i, 0, 0)), pl.BlockSpec((None, n, hw), lambda i: (i, 0, 0))),
        out_shape=(jax.ShapeDtypeStruct((b, hw, n), BF16), jax.ShapeDtypeStruct((b, n, hw), BF16)),
        compiler_params=_params(("parallel",)), name="mem_kv",
    )(mem, *consts)


def _aligned(start, align):
    return start if isinstance(start, int) else pl.multiple_of(start, align)


def _mla_kernel(qt_ref, k_ref, vt_ref, ot_ref, *scratch, tile_fn, tq):
    def body(i, carry):
        cols = pl.ds(pl.multiple_of(i * tq, tq), tq)
        ot_ref[:, cols] = tile_fn(qt_ref[:, cols], k_ref, vt_ref, *scratch)
        return carry

    lax.fori_loop(0, qt_ref.shape[1] // tq, body, 0)


def _mla_online_tile(qt, k_ref, vt_ref, s0_ref, s1_ref, p0_ref, p1_ref, *, tk, sub):
    s = k_ref.shape[0]
    tq = qt.shape[1]
    n = s // tk
    s_refs = (s0_ref, s1_ref)
    p_refs = (p0_ref, p1_ref)

    def step(c, par, carry, logits=True, accum=True, exps=True):
        mc, m, alpha, acc = carry
        if exps:
            m_new = jnp.maximum(m, mc)
            alpha_new = jnp.exp2(m - m_new)
        if accum:
            acc = alpha * acc
        mc_next = None
        for j in range(tk // sub):
            blk = slice(j * sub, (j + 1) * sub)
            if logits:
                off = _aligned((c + 1) * tk + j * sub, sub)
                st = _dot(k_ref[pl.ds(off, sub), :], qt)
                s_refs[1 - par][blk, :] = st
                mj = jnp.max(st, axis=0, keepdims=True)
                mc_next = mj if mc_next is None else jnp.maximum(mc_next, mj)
            if accum:
                off = _aligned((c - 1) * tk + j * sub, sub)
                acc = acc + _dot(vt_ref[:, pl.ds(off, sub)], p_refs[1 - par][blk, :])
            if exps:
                p_refs[par][blk, :] = jnp.exp2((s_refs[par][blk, :] - m_new).astype(BF16))
        return (mc_next if logits else mc, m_new if exps else m, alpha_new if exps else alpha, acc)

    def pair(i, carry):
        return step(2 * i + 2, 0, step(2 * i + 1, 1, carry))

    init = jnp.full((1, tq), NEG_BIG, F32)
    carry = (init, init, jnp.zeros((1, tq), F32), jnp.zeros((MLA_VROWS, tq), F32))
    carry = step(-1, 1, carry, accum=False, exps=False)
    carry = step(0, 0, carry, accum=False)
    carry = lax.fori_loop(0, n // 2 - 1, pair, carry, unroll=True)
    carry = step(n - 1, 1, carry, logits=False)
    _, _, _, acc = step(n, 0, carry, logits=False, exps=False)
    return (acc[0:MLA_V] / acc[MLA_V:MLA_V + 1]).astype(BF16)


def _mla_bounded_tile(qt, k_ref, vt_ref, *, sub):
    s = k_ref.shape[0]
    acc = jnp.zeros((MLA_VROWS, qt.shape[1]), F32)
    for j in range(s // sub):
        blk = slice(j * sub, (j + 1) * sub)
        pt = jnp.exp2(_dot(k_ref[blk, :], qt)).astype(BF16)
        acc = acc + _dot(vt_ref[:, blk], pt)
    return (acc[0:MLA_V] / acc[MLA_V:MLA_V + 1]).astype(BF16)


def _mla_attn(qt, k, vt, tq, tk, bounded):
    b, _, s = qt.shape
    tq_step = min(s, 4 * tq)
    grid = (b, MLA_HEADS, s // tq_step)
    assert (s // tk) % 2 == 0 and s % tq_step == 0
    if bounded:
        tile_fn = functools.partial(_mla_bounded_tile, sub=min(s, 2048))
        scratch = []
    else:
        tile_fn = functools.partial(_mla_online_tile, tk=tk, sub=min(tk, 256))
        scratch = [pltpu.VMEM((tk, tq), F32), pltpu.VMEM((tk, tq), F32),
                   pltpu.VMEM((tk, tq), BF16), pltpu.VMEM((tk, tq), BF16)]
    return pl.pallas_call(
        functools.partial(_mla_kernel, tile_fn=tile_fn, tq=tq), grid=grid,
        in_specs=[pl.BlockSpec((None, MLA_HP, tq_step), lambda i, h, j: (i, h, j)),
                  pl.BlockSpec((None, s, MLA_HP), lambda i, h, j: (i, 0, h)),
                  pl.BlockSpec((None, MLA_VROWS, s), lambda i, h, j: (i, h, 0))],
        out_specs=pl.BlockSpec((None, MLA_V, tq_step), lambda i, h, j: (i, h, j)),
        out_shape=jax.ShapeDtypeStruct((b, MLA_HEADS * MLA_V, s), BF16),
        scratch_shapes=scratch,
        compiler_params=_params(("parallel", "parallel", "arbitrary")),
        name="mla_attn_bounded" if bounded else "mla_attn",
    )(qt, k, vt)


SWA_SPAN = 3 * BLOCK


def _swa_bias():
    r = jnp.arange(SWA_SPAN)[:, None]
    c = jnp.arange(BLOCK)[None, :]
    return jnp.stack([jnp.where(jnp.abs(lead * BLOCK + c - r) <= WINDOW, 0.0, NEG_BIG) for lead in range(3)]).astype(F32)


def _swa_kernel(qt_ref, k_ref, vt_ref, sink_ref, bias_ref, ot_ref):
    s = k_ref.shape[0]
    t = qt_ref.shape[1]
    nb = s // BLOCK
    gw = SWA_GROUP * BLOCK
    zeros = jnp.zeros((SWA_HD, gw), BF16)
    ones = jnp.ones((16, SWA_SPAN), BF16)
    logits = []
    for blk in range(t // BLOCK):
        n = pl.program_id(1) * (t // BLOCK) + blk
        start = pl.multiple_of(jnp.clip((n - 1) * BLOCK, 0, s - SWA_SPAN), BLOCK)
        kwin = k_ref[pl.ds(start, SWA_SPAN), :]
        vwin = vt_ref[:, pl.ds(start, SWA_SPAN)]
        lead = jnp.where(n == 0, 0, jnp.where(n == nb - 1, 2, 1))
        bias = bias_ref[lead]
        bias = jnp.concatenate([bias] * SWA_GROUP, axis=1)
        for g in range(SWA_KV_HEADS):
            qg = jnp.concatenate(
                [qt_ref[(g * SWA_GROUP + j) * SWA_HD:(g * SWA_GROUP + j + 1) * SWA_HD, blk * BLOCK:(blk + 1) * BLOCK]
                 for j in range(SWA_GROUP)], axis=1)
            qpad = jnp.concatenate([qg, zeros] if g == 0 else [zeros, qg], axis=0)
            st = _dot(kwin, qpad) + bias
            vext = jnp.concatenate([vwin[g * SWA_HD:(g + 1) * SWA_HD], ones], axis=0)
            logits.append((blk, g, st, vext))
    weights = []
    for blk, g, st, vext in logits:
        sk = sink_ref[g:g + 1, :]
        m = jnp.maximum(jnp.max(st, axis=0, keepdims=True), sk)
        weights.append((blk, g, jnp.exp2((st - m).astype(BF16)), jnp.exp2(sk - m), vext))
    for blk, g, p, p_sink, vext in weights:
        acc = _dot(vext, p)
        ot = acc[0:SWA_HD] / (acc[SWA_HD:SWA_HD + 1] + p_sink)
        for j in range(SWA_GROUP):
            hd = g * SWA_GROUP + j
            ot_ref[hd * SWA_HD:(hd + 1) * SWA_HD, blk * BLOCK:(blk + 1) * BLOCK] = (
                ot[:, j * BLOCK:(j + 1) * BLOCK].astype(BF16))


def _swa_attn(qt, k, vt, sink_rows, t):
    b, hw, s = qt.shape
    kvw = SWA_KV_HEADS * SWA_HD
    bias = _swa_bias()
    return pl.pallas_call(
        _swa_kernel, grid=(b, s // t),
        in_specs=[pl.BlockSpec((None, hw, t), lambda i, j: (i, 0, j)),
                  pl.BlockSpec((None, s, kvw), lambda i, j: (i, 0, 0)),
                  pl.BlockSpec((None, kvw, s), lambda i, j: (i, 0, 0)),
                  _const_spec(sink_rows.shape), _const_spec(bias.shape)],
        out_specs=pl.BlockSpec((None, hw, t), lambda i, j: (i, 0, j)),
        out_shape=jax.ShapeDtypeStruct((b, hw, s), BF16),
        compiler_params=_params(("parallel", "arbitrary")), name="swa_attn",
    )(qt, k, vt, sink_rows, bias)


def _sigmoid(v):
    return 1.0 / (1.0 + jnp.exp(-v))


def _merge_kernel(x_ref, ot_mla_ref, ot_swa_ref, q_mem_ref, kt_mem_ref, v_mem_ref, gmix_ref, wg_ref,
                  wo_mla_ref, wo_swa_ref, wo_mem_ref, wout_ref, y_ref):
    x = x_ref[...]
    hb = (x * lax.rsqrt(jnp.mean(x * x, axis=-1, keepdims=True) + EPS) * gmix_ref[...]).astype(BF16)

    o_heads = []
    for hd in range(MEM_HEADS):
        q = q_mem_ref[:, hd * MEM_HD:(hd + 1) * MEM_HD]
        sc = _dot(q, kt_mem_ref[hd * MEM_HD:(hd + 1) * MEM_HD, :])
        p = jnp.exp2(sc - jnp.max(sc, axis=-1, keepdims=True))
        l = jnp.sum(p, axis=-1, keepdims=True)
        o_heads.append(_dot(p.astype(BF16), v_mem_ref[:, hd * MEM_HD:(hd + 1) * MEM_HD]) / l)
    o_mem = jnp.concatenate(o_heads, axis=1).astype(BF16)

    merged = _sigmoid(_dot(hb, wg_ref[:, 0:D_MODEL])) * _dot_tn(ot_mla_ref[...], wo_mla_ref[...])
    merged += _sigmoid(_dot(hb, wg_ref[:, D_MODEL:2 * D_MODEL])) * _dot_tn(ot_swa_ref[...], wo_swa_ref[...])
    merged += _sigmoid(_dot(hb, wg_ref[:, 2 * D_MODEL:3 * D_MODEL])) * _dot(o_mem, wo_mem_ref[...])
    y_ref[...] = x + _dot(merged.astype(BF16), wout_ref[...])


def _merge(x, ot_mla, ot_swa, q_mem, kt_mem, v_mem, w, tm):
    b, s, d = x.shape
    n_mem = v_mem.shape[1]
    hw = MEM_HEADS * MEM_HD
    tok = lambda width: pl.BlockSpec((None, tm, width), lambda i, j: (i, j, 0))
    tok_t = lambda rows: pl.BlockSpec((None, rows, tm), lambda i, j: (i, 0, j))
    consts = [w['g_mix'], w['wg'], w['wo_mla'], w['wo_swa'], w['wo_mem'], w['wout']]
    return pl.pallas_call(
        _merge_kernel, grid=(b, s // tm),
        in_specs=[tok(d), tok_t(MLA_HEADS * MLA_V), tok_t(SWA_HEADS * SWA_HD), tok(hw),
                  pl.BlockSpec((None, hw, n_mem), lambda i, j: (i, 0, 0)),
                  pl.BlockSpec((None, n_mem, hw), lambda i, j: (i, 0, 0))] + [_const_spec(c.shape) for c in consts],
        out_specs=tok(d), out_shape=jax.ShapeDtypeStruct((b, s, d), F32),
        compiler_params=_params(("parallel", "parallel")), name="merge",
    )(x, ot_mla, ot_swa, q_mem, kt_mem, v_mem, *consts)


HALO = 8


def _ffn_kernel(x_ref, prev_ref, next_ref, gffn_ref, wup_ref, cw_ref, cb_ref, wdown_ref, y_ref):
    j = pl.program_id(1)
    nj = pl.num_programs(1)
    x = x_ref[...]
    tm = x.shape[0]
    g = gffn_ref[...]

    def norm(v):
        return v * lax.rsqrt(jnp.mean(v * v, axis=-1, keepdims=True) + EPS) * g

    hp = jnp.where(j > 0, norm(prev_ref[...]), 0.0)
    hn = jnp.where(j < nj - 1, norm(next_ref[...]), 0.0)
    hext = jnp.concatenate([hp, norm(x), hn], axis=0).astype(BF16)
    u = _dot(hext, wup_ref[...])
    cw = cw_ref[...]
    conv = (u[HALO - 1:HALO - 1 + tm] * cw[0:1] + u[HALO:HALO + tm] * cw[1:2]
            + u[HALO + 1:HALO + 1 + tm] * cw[2:3] + cb_ref[...])
    a = conv[:, :D_FF]
    val = conv[:, D_FF:]
    act = (a * _sigmoid(a) * val).astype(BF16)
    y_ref[...] = x + _dot(act, wdown_ref[...])


def _ffn(x, w, tm):
    b, s, d = x.shape
    nh = tm // HALO
    last = s // HALO - 1
    consts = [w['g_ffn'], w['wup'], w['conv_w'], w['conv_b'], w['wdown']]
    return pl.pallas_call(
        _ffn_kernel, grid=(b, s // tm),
        in_specs=[pl.BlockSpec((None, tm, d), lambda i, j: (i, j, 0)),
                  pl.BlockSpec((None, HALO, d), lambda i, j: (i, jnp.maximum(j * nh - 1, 0), 0)),
                  pl.BlockSpec((None, HALO, d), lambda i, j: (i, jnp.minimum((j + 1) * nh, last), 0))]
                 + [_const_spec(c.shape) for c in consts],
        out_specs=pl.BlockSpec((None, tm, d), lambda i, j: (i, j, 0)),
        out_shape=jax.ShapeDtypeStruct((b, s, d), F32),
        compiler_params=_params(("parallel", "parallel")), name="ffn",
    )(x, x, x, *consts)


def _prep_weights(g_mix, g_mem, w_in, q_a_norm, w_q_b, kv_a_norm, w_kv_b, g_q_mla, g_k_mla, g_q_swa, g_k_swa,
                  swa_sink, w_mem_kv, g_q_mem, g_k_mem, w_o_mla, w_o_swa, w_o_mem, w_out, g_ffn, w_up, conv_w,
                  conv_b, w_down):
    offs = [0]
    for sp in SPLITS:
        offs.append(offs[-1] + sp)
    w_cq, w_ckv, w_kr, w_qs, w_ks, w_vs, w_qm, w_gate = (w_in[:, offs[i]:offs[i + 1]] for i in range(8))
    row = lambda v: v.reshape(1, -1).astype(F32)
    col = lambda v: v.reshape(-1, 1).astype(F32)
    w = {}
    w['g_mix'] = row(g_mix)
    w['g_mem'] = row(g_mem)
    w['g_ffn'] = row(g_ffn)
    w['q_a_norm'] = row(q_a_norm)
    w['kv_a_norm'] = row(kv_a_norm)
    w_kr_placed = jnp.pad(w_kr, ((0, 0), (MLA_NOPE, MLA_HP - MLA_QK)))
    w['wa'] = jnp.concatenate([w_cq, w_ckv, w_ks, w_qm, w_kr_placed], axis=1).astype(BF16)
    w['wbt'] = jnp.concatenate([w_qs, w_vs], axis=1).T.astype(BF16)
    w['wg'] = w_gate.astype(BF16)
    wq = jnp.pad(w_q_b.reshape(Q_LORA, MLA_HEADS, MLA_QK), ((0, 0), (0, 0), (0, MLA_HP - MLA_QK)))
    w['wqt'] = wq.reshape(Q_LORA, MLA_HEADS * MLA_HP).T.astype(BF16)
    wkv = w_kv_b.reshape(KV_LORA, MLA_HEADS, MLA_NOPE + MLA_V)
    wk_nope = jnp.pad(wkv[:, :, :MLA_NOPE], ((0, 0), (0, 0), (0, MLA_HP - MLA_NOPE))).reshape(KV_LORA, -1)
    w['wk'] = wk_nope.astype(BF16)
    w['wvt'] = wkv[:, :, MLA_NOPE:].reshape(KV_LORA, MLA_HEADS * MLA_V).T.astype(BF16)
    sc_mla = MLA_QK ** -0.5 * LOG2E
    w['gq_mla'] = col(jnp.pad(g_q_mla * sc_mla, (0, MLA_HP - MLA_QK)))
    w['gk_mla'] = row(jnp.pad(g_k_mla, (0, MLA_HP - MLA_QK)))
    w['mla_logit_bound'] = 1.02 * MLA_QK * jnp.max(jnp.abs(w['gq_mla'])) * jnp.max(jnp.abs(w['gk_mla']))
    w['gq_swa'] = col(g_q_swa * (SWA_HD ** -0.5 * LOG2E))
    w['gk_swa'] = row(jnp.tile(g_k_swa, SWA_KV_HEADS))
    w['gq_mem'] = row(g_q_mem * (MEM_HD ** -0.5 * LOG2E))
    w['gk_mem'] = col(g_k_mem)
    w['sink_rows'] = jnp.repeat(swa_sink.astype(F32) * LOG2E, BLOCK).reshape(SWA_KV_HEADS, SWA_GROUP * BLOCK)
    hw = MEM_HEADS * MEM_HD
    w['wmkt'] = w_mem_kv[:, :hw].T.astype(BF16)
    w['wmv'] = w_mem_kv[:, hw:].astype(BF16)
    w['wo_mla'] = w_o_mla.astype(BF16)
    w['wo_swa'] = w_o_swa.astype(BF16)
    w['wo_mem'] = w_o_mem.astype(BF16)
    w['wout'] = w_out.astype(BF16)
    w['wup'] = w_up.astype(BF16)
    w['wdown'] = w_down.astype(BF16)
    w['conv_w'] = conv_w.astype(F32)
    w['conv_b'] = row(conv_b)
    return w


def _rope_tables(s):
    def cs(dim):
        inv = 1.0 / (ROPE_THETA ** (jnp.arange(0, dim, 2, dtype=F32) / dim))
        ang = jnp.arange(s, dtype=F32)[:, None] * inv[None, :]
        return jnp.cos(ang), jnp.sin(ang)

    c16, s16 = cs(MLA_ROPE)
    c32, s32 = cs(SWA_HD)
    z = lambda n: jnp.zeros((s, n), F32)
    t = {'ct16': c16.T, 'st16': s16.T, 'ct32': c32.T, 'st32': s32.T}
    t['ck'] = jnp.concatenate([jnp.ones((s, MLA_NOPE), F32), c16, c16, z(MLA_HP - MLA_QK)], axis=1)
    t['sak'] = jnp.concatenate([z(MLA_NOPE), -s16, z(16), z(MLA_HP - MLA_QK)], axis=1)
    t['sbk'] = jnp.concatenate([z(MLA_NOPE), z(16), s16, z(MLA_HP - MLA_QK)], axis=1)
    t['cs'] = jnp.concatenate([c32, c32] * SWA_KV_HEADS, axis=1)
    t['sas'] = jnp.concatenate([-s32, z(32)] * SWA_KV_HEADS, axis=1)
    t['sbs'] = jnp.concatenate([z(32), s32] * SWA_KV_HEADS, axis=1)
    return t


def _tiles(s):
    pick = lambda want: want if s % want == 0 else BLOCK
    mla_k = 1024 if s % 2048 == 0 else BLOCK
    return dict(proj=pick(256), mla_q=pick(256), mla_k=mla_k, swa=pick(512), merge=pick(512), ffn=pick(512))


def _layer(x, mem, w, tabs):
    b, s, d = x.shape
    assert d == D_MODEL and s % BLOCK == 0 and s >= 3 * BLOCK
    ts = _tiles(s)
    qt_mla, k_mla, vt_mla, qt_swa, k_swa, vt_swa, q_mem = _proj(x, w, tabs, ts['proj'])
    kt_mem, v_mem = _mem_kv(mem, w)
    ot_mla = lax.cond(
        w['mla_logit_bound'] <= MLA_LOGIT_BOUND,
        lambda: _mla_attn(qt_mla, k_mla, vt_mla, ts['mla_q'], ts['mla_k'], True),
        lambda: _mla_attn(qt_mla, k_mla, vt_mla, ts['mla_q'], ts['mla_k'], False))
    ot_swa = _swa_attn(qt_swa, k_swa, vt_swa, w['sink_rows'], ts['swa'])
    x1 = _merge(x, ot_mla, ot_swa, q_mem, kt_mem, v_mem, w, ts['merge'])
    return _ffn(x1, w, ts['ffn'])


def kernel(x_prompt, x_sample, mem_prompt, mem_sample, g_mix, g_mem, w_in, q_a_norm, w_q_b, kv_a_norm, w_kv_b,
           g_q_mla, g_k_mla, g_q_swa, g_k_swa, swa_sink, w_mem_kv, g_q_mem, g_k_mem, w_o_mla, w_o_swa, w_o_mem,
           w_out, g_ffn, w_up, conv_w, conv_b, w_down):
    weights = (g_mix, g_mem, w_in, q_a_norm, w_q_b, kv_a_norm, w_kv_b, g_q_mla, g_k_mla, g_q_swa, g_k_swa,
               swa_sink, w_mem_kv, g_q_mem, g_k_mem, w_o_mla, w_o_swa, w_o_mem, w_out, g_ffn, w_up, conv_w,
               conv_b, w_down)
    depth = g_mix.shape[0]
    y_prompt, y_sample = x_prompt, x_sample
    tabs = _rope_tables(max(x_prompt.shape[1], x_sample.shape[1]))
    for layer in range(depth):
        w = _prep_weights(*(p[layer] for p in weights))
        y_prompt = _layer(y_prompt, mem_prompt, w, tabs)
        y_sample = _layer(y_sample, mem_sample, w, tabs)
    return (y_prompt, y_sample)
```

```python
import functools
import math

import jax
import jax.numpy as jnp
from jax import lax
from jax.experimental import pallas as pl
from jax.experimental.pallas import tpu as pltpu

D_MODEL = 1024
N_MEM = 256
EPS = 1e-6
ROPE_THETA = 10000.0
MLA_HEADS = 8
MLA_NOPE = 64
MLA_ROPE = 32
MLA_V = 64
MLA_QK = MLA_NOPE + MLA_ROPE
Q_LORA = 384
KV_LORA = 256
SWA_HEADS = 8
SWA_KV_HEADS = 2
SWA_GROUP = SWA_HEADS // SWA_KV_HEADS
SWA_HD = 64
WINDOW = 128
BLOCK = 128
MEM_HEADS = 4
MEM_HD = 128
N_BRANCH = 3
D_FF = 2816
SPLITS = (Q_LORA, KV_LORA, MLA_ROPE, SWA_HEADS * SWA_HD, SWA_KV_HEADS * SWA_HD,
          SWA_KV_HEADS * SWA_HD, MEM_HEADS * MEM_HD, N_BRANCH * D_MODEL)

LANE = 128
MLA_HP = 128
MLA_VROWS = 80
LOG2E = math.log2(math.e)
NEG_BIG = -1e30
MLA_LOGIT_BOUND = 80.0
VMEM_LIMIT = 56 * 1024 * 1024

BF16 = jnp.bfloat16
F32 = jnp.float32


def _dot(a, b):
    return jnp.dot(a, b, preferred_element_type=F32)


def _dot_nt(a, b):
    return lax.dot_general(a, b, (((1,), (1,)), ((), ())), preferred_element_type=F32)


def _dot_tn(a, b):
    return lax.dot_general(a, b, (((0,), (0,)), ((), ())), preferred_element_type=F32)


def _const_spec(shape):
    nd = len(shape)
    return pl.BlockSpec(shape, lambda *_: (0,) * nd, pipeline_mode=pl.Buffered(1))


def _params(semantics):
    return pltpu.CompilerParams(dimension_semantics=semantics, vmem_limit_bytes=VMEM_LIMIT)


def _proj_kernel(x_ref, gmix_ref, wa_ref, wbt_ref, qan_ref, wqt_ref, kvan_ref, wk_ref, wvt_ref,
                 gq_mla_ref, gk_mla_ref, gq_swa_ref, gk_swa_ref, gq_mem_ref,
                 ct16_ref, st16_ref, ck_ref, sak_ref, sbk_ref,
                 ct32_ref, st32_ref, cs_ref, sas_ref, sbs_ref,
                 qt_mla_ref, k_mla_ref, vt_mla_ref, qt_swa_ref, k_swa_ref, vt_swa_ref, q_mem_ref):
    x = x_ref[...]
    tm = x.shape[0]
    h = x * lax.rsqrt(jnp.mean(x * x, axis=-1, keepdims=True) + EPS) * gmix_ref[...]
    hb = h.astype(BF16)
    z = _dot(hb, wa_ref[...])
    c_q = z[:, 0:384]
    c_kv = z[:, 384:640]
    k_s = z[:, 640:768]
    q_m = z[:, 768:1280]
    k_rope = z[:, 1280:1408]

    bt = _dot_nt(wbt_ref[...], hb)
    c_qn = (c_q * lax.rsqrt(jnp.mean(c_q * c_q, axis=-1, keepdims=True) + EPS) * qan_ref[...]).astype(BF16)
    c_kvn = (c_kv * lax.rsqrt(jnp.mean(c_kv * c_kv, axis=-1, keepdims=True) + EPS) * kvan_ref[...]).astype(BF16)
    qt = _dot_nt(wqt_ref[...], c_qn)
    kpre = _dot(c_kvn, wk_ref[...])
    vt = _dot_nt(wvt_ref[...], c_kvn)

    ct = ct16_ref[...]
    st = st16_ref[...]
    gq = gq_mla_ref[...]
    for hd in range(MLA_HEADS):
        blk = qt[hd * MLA_HP:(hd + 1) * MLA_HP]
        ss = jnp.sum(blk * blk, axis=0, keepdims=True)
        y = blk * lax.rsqrt(ss * (1.0 / MLA_QK) + EPS) * gq
        x1 = y[64:80]
        x2 = y[80:96]
        out = jnp.concatenate([y[0:64], x1 * ct - x2 * st, x2 * ct + x1 * st, y[96:128]], axis=0)
        qt_mla_ref[hd * MLA_HP:(hd + 1) * MLA_HP, :] = out.astype(BF16)

    gk = gk_mla_ref[...]
    ss_rope = jnp.sum(k_rope * k_rope, axis=-1, keepdims=True)
    yr = k_rope * gk
    k_roped = yr * ck_ref[...] + pltpu.roll(yr, 112, 1) * sak_ref[...] + pltpu.roll(yr, 16, 1) * sbk_ref[...]
    for hd in range(MLA_HEADS):
        blk = kpre[:, hd * MLA_HP:(hd + 1) * MLA_HP]
        ss = jnp.sum(blk * blk, axis=-1, keepdims=True) + ss_rope
        out = (blk * gk + k_roped) * lax.rsqrt(ss * (1.0 / MLA_QK) + EPS)
        k_mla_ref[hd] = out.astype(BF16)
    ones = jnp.ones((MLA_VROWS - MLA_V, tm), BF16)
    for hd in range(MLA_HEADS):
        vt_mla_ref[hd * MLA_VROWS:hd * MLA_VROWS + MLA_V, :] = vt[hd * MLA_V:(hd + 1) * MLA_V].astype(BF16)
        vt_mla_ref[hd * MLA_VROWS + MLA_V:(hd + 1) * MLA_VROWS, :] = ones

    c32 = ct32_ref[...]
    s32 = st32_ref[...]
    gqs = gq_swa_ref[...]
    for hd in range(SWA_HEADS):
        blk = bt[hd * SWA_HD:(hd + 1) * SWA_HD]
        ss = jnp.sum(blk * blk, axis=0, keepdims=True)
        y = blk * lax.rsqrt(ss * (1.0 / SWA_HD) + EPS) * gqs
        x1 = y[0:32]
        x2 = y[32:64]
        out = jnp.concatenate([x1 * c32 - x2 * s32, x2 * c32 + x1 * s32], axis=0)
        qt_swa_ref[hd * SWA_HD:(hd + 1) * SWA_HD, :] = out.astype(BF16)
    vt_swa_ref[...] = bt[SWA_HEADS * SWA_HD:].astype(BF16)
    lane = lax.broadcasted_iota(jnp.int32, (tm, LANE), 1)
    lo = lane < SWA_HD
    sq = k_s * k_s
    ss_lo = jnp.sum(jnp.where(lo, sq, 0.0), axis=-1, keepdims=True)
    ss_hi = jnp.sum(jnp.where(lo, 0.0, sq), axis=-1, keepdims=True)
    rstd = jnp.where(lo, lax.rsqrt(ss_lo * (1.0 / SWA_HD) + EPS), lax.rsqrt(ss_hi * (1.0 / SWA_HD) + EPS))
    y = k_s * rstd * gk_swa_ref[...]
    out = y * cs_ref[...] + pltpu.roll(y, 96, 1) * sas_ref[...] + pltpu.roll(y, 32, 1) * sbs_ref[...]
    k_swa_ref[...] = out.astype(BF16)

    gqm = gq_mem_ref[...]
    for hd in range(MEM_HEADS):
        blk = q_m[:, hd * MEM_HD:(hd + 1) * MEM_HD]
        ss = jnp.sum(blk * blk, axis=-1, keepdims=True)
        y = blk * lax.rsqrt(ss * (1.0 / MEM_HD) + EPS) * gqm
        q_mem_ref[:, hd * MEM_HD:(hd + 1) * MEM_HD] = y.astype(BF16)


def _proj(x, w, tabs, tm):
    b, s, d = x.shape
    grid = (s // tm, b)
    tok = lambda width: pl.BlockSpec((None, tm, width), lambda j, i: (i, j, 0))
    tok_t = lambda rows: pl.BlockSpec((None, rows, tm), lambda j, i: (i, 0, j))
    tab = lambda width: pl.BlockSpec((tm, width), lambda j, i: (j, 0))
    tab_t = lambda rows: pl.BlockSpec((rows, tm), lambda j, i: (0, j))
    consts = [w['g_mix'], w['wa'], w['wbt'], w['q_a_norm'], w['wqt'], w['kv_a_norm'], w['wk'], w['wvt'],
              w['gq_mla'], w['gk_mla'], w['gq_swa'], w['gk_swa'], w['gq_mem']]
    in_specs = [tok(d)] + [_const_spec(c.shape) for c in consts] + [
        tab_t(16), tab_t(16), tab(LANE), tab(LANE), tab(LANE),
        tab_t(32), tab_t(32), tab(LANE), tab(LANE), tab(LANE)]
    out_shape = (
        jax.ShapeDtypeStruct((b, MLA_HEADS * MLA_HP, s), BF16),
        jax.ShapeDtypeStruct((b, MLA_HEADS, s, MLA_HP), BF16),
        jax.ShapeDtypeStruct((b, MLA_HEADS * MLA_VROWS, s), BF16),
        jax.ShapeDtypeStruct((b, SWA_HEADS * SWA_HD, s), BF16),
        jax.ShapeDtypeStruct((b, s, SWA_KV_HEADS * SWA_HD), BF16),
        jax.ShapeDtypeStruct((b, SWA_KV_HEADS * SWA_HD, s), BF16),
        jax.ShapeDtypeStruct((b, s, MEM_HEADS * MEM_HD), BF16),
    )
    k_heads = pl.BlockSpec((None, MLA_HEADS, tm, MLA_HP), lambda j, i: (i, 0, j, 0))
    out_specs = (tok_t(MLA_HEADS * MLA_HP), k_heads, tok_t(MLA_HEADS * MLA_VROWS),
                 tok_t(SWA_HEADS * SWA_HD), tok(SWA_KV_HEADS * SWA_HD), tok_t(SWA_KV_HEADS * SWA_HD),
                 tok(MEM_HEADS * MEM_HD))
    return pl.pallas_call(
        _proj_kernel, grid=grid, in_specs=in_specs, out_specs=out_specs, out_shape=out_shape,
        compiler_params=_params(("parallel", "parallel")), name="proj",
    )(x, *consts, tabs['ct16'], tabs['st16'], tabs['ck'], tabs['sak'], tabs['sbk'],
      tabs['ct32'], tabs['st32'], tabs['cs'], tabs['sas'], tabs['sbs'])


def _mem_kv_kernel(mem_ref, gmem_ref, wkt_ref, wv_ref, gk_ref, kt_ref, v_ref):
    m = mem_ref[...]
    mn = (m * lax.rsqrt(jnp.mean(m * m, axis=-1, keepdims=True) + EPS) * gmem_ref[...]).astype(BF16)
    kt = _dot_nt(wkt_ref[...], mn)
    gk = gk_ref[...]
    for hd in range(MEM_HEADS):
        blk = kt[hd * MEM_HD:(hd + 1) * MEM_HD]
        ss = jnp.sum(blk * blk, axis=0, keepdims=True)
        kt_ref[hd * MEM_HD:(hd + 1) * MEM_HD, :] = (blk * lax.rsqrt(ss * (1.0 / MEM_HD) + EPS) * gk).astype(BF16)
    v_ref[...] = _dot(mn, wv_ref[...]).astype(BF16)


def _mem_kv(mem, w):
    b, n, d = mem.shape
    hw = MEM_HEADS * MEM_HD
    consts = [w['g_mem'], w['wmkt'], w['wmv'], w['gk_mem']]
    return pl.pallas_call(
        _mem_kv_kernel, grid=(b,),
        in_specs=[pl.BlockSpec((None, n, d), lambda i: (i, 0, 0))] + [_const_spec(c.shape) for c in consts],
        out_specs=(pl.BlockSpec((None, hw, n), lambda i: (i, 0, 0)), pl.BlockSpec((None, n, hw), lambda i: (i, 0, 0))),
        out_shape=(jax.ShapeDtypeStruct((b, hw, n), BF16), jax.ShapeDtypeStruct((b, n, hw), BF16)),
        compiler_params=_params(("parallel",)), name="mem_kv",
    )(mem, *consts)


def _aligned(start, align):
    return start if isinstance(start, int) else pl.multiple_of(start, align)


def _mla_kernel(qt_ref, k_ref, vt_ref, ot_ref, *scratch, tile_fn, tq):
    def body(i, carry):
        cols = pl.ds(pl.multiple_of(i * tq, tq), tq)
        ot_ref[:, cols] = tile_fn(qt_ref[:, cols], k_ref, vt_ref, *scratch)
        return carry

    lax.fori_loop(0, qt_ref.shape[1] // tq, body, 0)


def _mla_online_tile(qt, k_ref, vt_ref, s0_ref, s1_ref, p0_ref, p1_ref, *, tk, sub):
    s = k_ref.shape[0]
    tq = qt.shape[1]
    n = s // tk
    s_refs = (s0_ref, s1_ref)
    p_refs = (p0_ref, p1_ref)

    def step(c, par, carry, logits=True, accum=True, exps=True):
        mc, m, alpha, acc = carry
        if exps:
            m_new = jnp.maximum(m, mc)
            alpha_new = jnp.exp2(m - m_new)
        if accum:
            acc = alpha * acc
        mc_next = None
        for j in range(tk // sub):
            blk = slice(j * sub, (j + 1) * sub)
            if logits:
                off = _aligned((c + 1) * tk + j * sub, sub)
                st = _dot(k_ref[pl.ds(off, sub), :], qt)
                s_refs[1 - par][blk, :] = st
                mj = jnp.max(st, axis=0, keepdims=True)
                mc_next = mj if mc_next is None else jnp.maximum(mc_next, mj)
            if accum:
                off = _aligned((c - 1) * tk + j * sub, sub)
                acc = acc + _dot(vt_ref[:, pl.ds(off, sub)], p_refs[1 - par][blk, :])
            if exps:
                p_refs[par][blk, :] = jnp.exp2((s_refs[par][blk, :] - m_new).astype(BF16))
        return (mc_next if logits else mc, m_new if exps else m, alpha_new if exps else alpha, acc)

    def pair(i, carry):
        return step(2 * i + 2, 0, step(2 * i + 1, 1, carry))

    init = jnp.full((1, tq), NEG_BIG, F32)
    carry = (init, init, jnp.zeros((1, tq), F32), jnp.zeros((MLA_VROWS, tq), F32))
    carry = step(-1, 1, carry, accum=False, exps=False)
    carry = step(0, 0, carry, accum=False)
    carry = lax.fori_loop(0, n // 2 - 1, pair, carry, unroll=True)
    carry = step(n - 1, 1, carry, logits=False)
    _, _, _, acc = step(n, 0, carry, logits=False, exps=False)
    return (acc[0:MLA_V] / acc[MLA_V:MLA_V + 1]).astype(BF16)


def _mla_bounded_tile(qt, k_ref, vt_ref, *, sub):
    s = k_ref.shape[0]
    acc = jnp.zeros((MLA_VROWS, qt.shape[1]), F32)
    for j in range(s // sub):
        blk = slice(j * sub, (j + 1) * sub)
        pt = jnp.exp2(_dot(k_ref[blk, :], qt)).astype(BF16)
        acc = acc + _dot(vt_ref[:, blk], pt)
    return (acc[0:MLA_V] / acc[MLA_V:MLA_V + 1]).astype(BF16)


def _mla_attn(qt, k, vt, tq, tk, bounded):
    b, _, s = qt.shape
    tq_step = min(s, 4 * tq)
    grid = (b, MLA_HEADS, s // tq_step)
    assert (s // tk) % 2 == 0 and s % tq_step == 0
    if bounded:
        tile_fn = functools.partial(_mla_bounded_tile, sub=min(s, 2048))
        scratch = []
    else:
        tile_fn = functools.partial(_mla_online_tile, tk=tk, sub=min(tk, 256))
        scratch = [pltpu.VMEM((tk, tq), F32), pltpu.VMEM((tk, tq), F32),
                   pltpu.VMEM((tk, tq), BF16), pltpu.VMEM((tk, tq), BF16)]
    return pl.pallas_call(
        functools.partial(_mla_kernel, tile_fn=tile_fn, tq=tq), grid=grid,
        in_specs=[pl.BlockSpec((None, MLA_HP, tq_step), lambda i, h, j: (i, h, j)),
                  pl.BlockSpec((None, None, s, MLA_HP), lambda i, h, j: (i, h, 0, 0)),
                  pl.BlockSpec((None, MLA_VROWS, s), lambda i, h, j: (i, h, 0))],
        out_specs=pl.BlockSpec((None, MLA_V, tq_step), lambda i, h, j: (i, h, j)),
        out_shape=jax.ShapeDtypeStruct((b, MLA_HEADS * MLA_V, s), BF16),
        scratch_shapes=scratch,
        compiler_params=_params(("parallel", "parallel", "arbitrary")),
        name="mla_attn_bounded" if bounded else "mla_attn",
    )(qt, k, vt)


SWA_SPAN = 3 * BLOCK


def _swa_bias():
    r = jnp.arange(SWA_SPAN)[:, None]
    c = jnp.arange(BLOCK)[None, :]
    return jnp.stack([jnp.where(jnp.abs(lead * BLOCK + c - r) <= WINDOW, 0.0, NEG_BIG) for lead in range(3)]).astype(F32)


def _swa_kernel(qt_ref, k_ref, vt_ref, sink_ref, bias_ref, ot_ref):
    s = k_ref.shape[0]
    t = qt_ref.shape[1]
    nb = s // BLOCK
    gw = SWA_GROUP * BLOCK
    zeros = jnp.zeros((SWA_HD, gw), BF16)
    ones = jnp.ones((16, SWA_SPAN), BF16)
    logits = []
    for blk in range(t // BLOCK):
        n = pl.program_id(1) * (t // BLOCK) + blk
        start = pl.multiple_of(jnp.clip((n - 1) * BLOCK, 0, s - SWA_SPAN), BLOCK)
        kwin = k_ref[pl.ds(start, SWA_SPAN), :]
        vwin = vt_ref[:, pl.ds(start, SWA_SPAN)]
        lead = jnp.where(n == 0, 0, jnp.where(n == nb - 1, 2, 1))
        bias = bias_ref[lead]
        bias = jnp.concatenate([bias] * SWA_GROUP, axis=1)
        for g in range(SWA_KV_HEADS):
            qg = jnp.concatenate(
                [qt_ref[(g * SWA_GROUP + j) * SWA_HD:(g * SWA_GROUP + j + 1) * SWA_HD, blk * BLOCK:(blk + 1) * BLOCK]
                 for j in range(SWA_GROUP)], axis=1)
            qpad = jnp.concatenate([qg, zeros] if g == 0 else [zeros, qg], axis=0)
            st = _dot(kwin, qpad) + bias
            vext = jnp.concatenate([vwin[g * SWA_HD:(g + 1) * SWA_HD], ones], axis=0)
            logits.append((blk, g, st, vext))
    weights = []
    for blk, g, st, vext in logits:
        sk = sink_ref[g:g + 1, :]
        m = jnp.maximum(jnp.max(st, axis=0, keepdims=True), sk)
        weights.append((blk, g, jnp.exp2((st - m).astype(BF16)), jnp.exp2(sk - m), vext))
    for blk, g, p, p_sink, vext in weights:
        acc = _dot(vext, p)
        ot = acc[0:SWA_HD] / (acc[SWA_HD:SWA_HD + 1] + p_sink)
        for j in range(SWA_GROUP):
            hd = g * SWA_GROUP + j
            ot_ref[hd * SWA_HD:(hd + 1) * SWA_HD, blk * BLOCK:(blk + 1) * BLOCK] = (
                ot[:, j * BLOCK:(j + 1) * BLOCK].astype(BF16))


def _swa_attn(qt, k, vt, sink_rows, t):
    b, hw, s = qt.shape
    kvw = SWA_KV_HEADS * SWA_HD
    bias = _swa_bias()
    return pl.pallas_call(
        _swa_kernel, grid=(b, s // t),
        in_specs=[pl.BlockSpec((None, hw, t), lambda i, j: (i, 0, j)),
                  pl.BlockSpec((None, s, kvw), lambda i, j: (i, 0, 0)),
                  pl.BlockSpec((None, kvw, s), lambda i, j: (i, 0, 0)),
                  _const_spec(sink_rows.shape), _const_spec(bias.shape)],
        out_specs=pl.BlockSpec((None, hw, t), lambda i, j: (i, 0, j)),
        out_shape=jax.ShapeDtypeStruct((b, hw, s), BF16),
        compiler_params=_params(("parallel", "arbitrary")), name="swa_attn",
    )(qt, k, vt, sink_rows, bias)


def _sigmoid(v):
    return 1.0 / (1.0 + jnp.exp(-v))


def _merge_kernel(x_ref, ot_mla_ref, ot_swa_ref, q_mem_ref, kt_mem_ref, v_mem_ref, gmix_ref, wg_ref,
                  wo_mla_ref, wo_swa_ref, wo_mem_ref, wout_ref, y_ref):
    x = x_ref[...]
    hb = (x * lax.rsqrt(jnp.mean(x * x, axis=-1, keepdims=True) + EPS) * gmix_ref[...]).astype(BF16)

    heads = [slice(hd * MEM_HD, (hd + 1) * MEM_HD) for hd in range(MEM_HEADS)]
    logits = [_dot(q_mem_ref[:, hs], kt_mem_ref[hs, :]) for hs in heads]
    probs = [jnp.exp2(sc - jnp.max(sc, axis=-1, keepdims=True)) for sc in logits]
    o_heads = [_dot(p.astype(BF16), v_mem_ref[:, hs]) / jnp.sum(p, axis=-1, keepdims=True)
               for p, hs in zip(probs, heads)]
    o_mem = jnp.concatenate(o_heads, axis=1).astype(BF16)

    merged = _sigmoid(_dot(hb, wg_ref[:, 0:D_MODEL])) * _dot_tn(ot_mla_ref[...], wo_mla_ref[...])
    merged += _sigmoid(_dot(hb, wg_ref[:, D_MODEL:2 * D_MODEL])) * _dot_tn(ot_swa_ref[...], wo_swa_ref[...])
    merged += _sigmoid(_dot(hb, wg_ref[:, 2 * D_MODEL:3 * D_MODEL])) * _dot(o_mem, wo_mem_ref[...])
    y_ref[...] = x + _dot(merged.astype(BF16), wout_ref[...])


def _merge(x, ot_mla, ot_swa, q_mem, kt_mem, v_mem, w, tm):
    b, s, d = x.shape
    n_mem = v_mem.shape[1]
    hw = MEM_HEADS * MEM_HD
    tok = lambda width: pl.BlockSpec((None, tm, width), lambda i, j: (i, j, 0))
    tok_t = lambda rows: pl.BlockSpec((None, rows, tm), lambda i, j: (i, 0, j))
    consts = [w['g_mix'], w['wg'], w['wo_mla'], w['wo_swa'], w['wo_mem'], w['wout']]
    return pl.pallas_call(
        _merge_kernel, grid=(b, s // tm),
        in_specs=[tok(d), tok_t(MLA_HEADS * MLA_V), tok_t(SWA_HEADS * SWA_HD), tok(hw),
                  pl.BlockSpec((None, hw, n_mem), lambda i, j: (i, 0, 0)),
                  pl.BlockSpec((None, n_mem, hw), lambda i, j: (i, 0, 0))] + [_const_spec(c.shape) for c in consts],
        out_specs=tok(d), out_shape=jax.ShapeDtypeStruct((b, s, d), F32),
        compiler_params=_params(("parallel", "parallel")), name="merge",
    )(x, ot_mla, ot_swa, q_mem, kt_mem, v_mem, *consts)


HALO = 8


def _ffn_kernel(x_ref, prev_ref, next_ref, gffn_ref, wup_ref, cw_ref, cb_ref, wdown_ref, y_ref):
    j = pl.program_id(1)
    nj = pl.num_programs(1)
    x = x_ref[...]
    tm = x.shape[0]
    g = gffn_ref[...]

    def norm(v):
        return v * lax.rsqrt(jnp.mean(v * v, axis=-1, keepdims=True) + EPS) * g

    hp = jnp.where(j > 0, norm(prev_ref[...]), 0.0)
    hn = jnp.where(j < nj - 1, norm(next_ref[...]), 0.0)
    hext = jnp.concatenate([hp, norm(x), hn], axis=0).astype(BF16)
    u = _dot(hext, wup_ref[...])
    cw = cw_ref[...]
    conv = (u[HALO - 1:HALO - 1 + tm] * cw[0:1] + u[HALO:HALO + tm] * cw[1:2]
            + u[HALO + 1:HALO + 1 + tm] * cw[2:3] + cb_ref[...])
    a = conv[:, :D_FF]
    val = conv[:, D_FF:]
    act = (a * _sigmoid(a) * val).astype(BF16)
    y_ref[...] = x + _dot(act, wdown_ref[...])


def _ffn(x, w, tm):
    b, s, d = x.shape
    nh = tm // HALO
    last = s // HALO - 1
    consts = [w['g_ffn'], w['wup'], w['conv_w'], w['conv_b'], w['wdown']]
    return pl.pallas_call(
        _ffn_kernel, grid=(b, s // tm),
        in_specs=[pl.BlockSpec((None, tm, d), lambda i, j: (i, j, 0)),
                  pl.BlockSpec((None, HALO, d), lambda i, j: (i, jnp.maximum(j * nh - 1, 0), 0)),
                  pl.BlockSpec((None, HALO, d), lambda i, j: (i, jnp.minimum((j + 1) * nh, last), 0))]
                 + [_const_spec(c.shape) for c in consts],
        out_specs=pl.BlockSpec((None, tm, d), lambda i, j: (i, j, 0)),
        out_shape=jax.ShapeDtypeStruct((b, s, d), F32),
        compiler_params=_params(("parallel", "parallel")), name="ffn",
    )(x, x, x, *consts)


def _prep_weights(g_mix, g_mem, w_in, q_a_norm, w_q_b, kv_a_norm, w_kv_b, g_q_mla, g_k_mla, g_q_swa, g_k_swa,
                  swa_sink, w_mem_kv, g_q_mem, g_k_mem, w_o_mla, w_o_swa, w_o_mem, w_out, g_ffn, w_up, conv_w,
                  conv_b, w_down):
    offs = [0]
    for sp in SPLITS:
        offs.append(offs[-1] + sp)
    w_cq, w_ckv, w_kr, w_qs, w_ks, w_vs, w_qm, w_gate = (w_in[:, offs[i]:offs[i + 1]] for i in range(8))
    row = lambda v: v.reshape(1, -1).astype(F32)
    col = lambda v: v.reshape(-1, 1).astype(F32)
    w = {}
    w['g_mix'] = row(g_mix)
    w['g_mem'] = row(g_mem)
    w['g_ffn'] = row(g_ffn)
    w['q_a_norm'] = row(q_a_norm)
    w['kv_a_norm'] = row(kv_a_norm)
    w_kr_placed = jnp.pad(w_kr, ((0, 0), (MLA_NOPE, MLA_HP - MLA_QK)))
    w['wa'] = jnp.concatenate([w_cq, w_ckv, w_ks, w_qm, w_kr_placed], axis=1).astype(BF16)
    w['wbt'] = jnp.concatenate([w_qs, w_vs], axis=1).T.astype(BF16)
    w['wg'] = w_gate.astype(BF16)
    wq = jnp.pad(w_q_b.reshape(Q_LORA, MLA_HEADS, MLA_QK), ((0, 0), (0, 0), (0, MLA_HP - MLA_QK)))
    w['wqt'] = wq.reshape(Q_LORA, MLA_HEADS * MLA_HP).T.astype(BF16)
    wkv = w_kv_b.reshape(KV_LORA, MLA_HEADS, MLA_NOPE + MLA_V)
    wk_nope = jnp.pad(wkv[:, :, :MLA_NOPE], ((0, 0), (0, 0), (0, MLA_HP - MLA_NOPE))).reshape(KV_LORA, -1)
    w['wk'] = wk_nope.astype(BF16)
    w['wvt'] = wkv[:, :, MLA_NOPE:].reshape(KV_LORA, MLA_HEADS * MLA_V).T.astype(BF16)
    sc_mla = MLA_QK ** -0.5 * LOG2E
    w['gq_mla'] = col(jnp.pad(g_q_mla * sc_mla, (0, MLA_HP - MLA_QK)))
    w['gk_mla'] = row(jnp.pad(g_k_mla, (0, MLA_HP - MLA_QK)))
    w['mla_logit_bound'] = 1.02 * MLA_QK * jnp.max(jnp.abs(w['gq_mla'])) * jnp.max(jnp.abs(w['gk_mla']))
    w['gq_swa'] = col(g_q_swa * (SWA_HD ** -0.5 * LOG2E))
    w['gk_swa'] = row(jnp.tile(g_k_swa, SWA_KV_HEADS))
    w['gq_mem'] = row(g_q_mem * (MEM_HD ** -0.5 * LOG2E))
    w['gk_mem'] = col(g_k_mem)
    w['sink_rows'] = jnp.repeat(swa_sink.astype(F32) * LOG2E, BLOCK).reshape(SWA_KV_HEADS, SWA_GROUP * BLOCK)
    hw = MEM_HEADS * MEM_HD
    w['wmkt'] = w_mem_kv[:, :hw].T.astype(BF16)
    w['wmv'] = w_mem_kv[:, hw:].astype(BF16)
    w['wo_mla'] = w_o_mla.astype(BF16)
    w['wo_swa'] = w_o_swa.astype(BF16)
    w['wo_mem'] = w_o_mem.astype(BF16)
    w['wout'] = w_out.astype(BF16)
    w['wup'] = w_up.astype(BF16)
    w['wdown'] = w_down.astype(BF16)
    w['conv_w'] = conv_w.astype(F32)
    w['conv_b'] = row(conv_b)
    return w


def _rope_tables(s):
    def cs(dim):
        inv = 1.0 / (ROPE_THETA ** (jnp.arange(0, dim, 2, dtype=F32) / dim))
        ang = jnp.arange(s, dtype=F32)[:, None] * inv[None, :]
        return jnp.cos(ang), jnp.sin(ang)

    c16, s16 = cs(MLA_ROPE)
    c32, s32 = cs(SWA_HD)
    z = lambda n: jnp.zeros((s, n), F32)
    t = {'ct16': c16.T, 'st16': s16.T, 'ct32': c32.T, 'st32': s32.T}
    t['ck'] = jnp.concatenate([jnp.ones((s, MLA_NOPE), F32), c16, c16, z(MLA_HP - MLA_QK)], axis=1)
    t['sak'] = jnp.concatenate([z(MLA_NOPE), -s16, z(16), z(MLA_HP - MLA_QK)], axis=1)
    t['sbk'] = jnp.concatenate([z(MLA_NOPE), z(16), s16, z(MLA_HP - MLA_QK)], axis=1)
    t['cs'] = jnp.concatenate([c32, c32] * SWA_KV_HEADS, axis=1)
    t['sas'] = jnp.concatenate([-s32, z(32)] * SWA_KV_HEADS, axis=1)
    t['sbs'] = jnp.concatenate([z(32), s32] * SWA_KV_HEADS, axis=1)
    return t


def _tiles(s):
    pick = lambda want: want if s % want == 0 else BLOCK
    mla_k = 1024 if s % 2048 == 0 else BLOCK
    return dict(proj=pick(256), mla_q=pick(256), mla_k=mla_k, swa=pick(512), merge=pick(512), ffn=pick(512))


def _layer(x, mem, w, tabs):
    b, s, d = x.shape
    assert d == D_MODEL and s % BLOCK == 0 and s >= 3 * BLOCK
    ts = _tiles(s)
    qt_mla, k_mla, vt_mla, qt_swa, k_swa, vt_swa, q_mem = _proj(x, w, tabs, ts['proj'])
    kt_mem, v_mem = _mem_kv(mem, w)
    ot_mla = lax.cond(
        w['mla_logit_bound'] <= MLA_LOGIT_BOUND,
        lambda: _mla_attn(qt_mla, k_mla, vt_mla, ts['mla_q'], ts['mla_k'], True),
        lambda: _mla_attn(qt_mla, k_mla, vt_mla, ts['mla_q'], ts['mla_k'], False))
    ot_swa = _swa_attn(qt_swa, k_swa, vt_swa, w['sink_rows'], ts['swa'])
    x1 = _merge(x, ot_mla, ot_swa, q_mem, kt_mem, v_mem, w, ts['merge'])
    return _ffn(x1, w, ts['ffn'])


def kernel(x_prompt, x_sample, mem_prompt, mem_sample, g_mix, g_mem, w_in, q_a_norm, w_q_b, kv_a_norm, w_kv_b,
           g_q_mla, g_k_mla, g_q_swa, g_k_swa, swa_sink, w_mem_kv, g_q_mem, g_k_mem, w_o_mla, w_o_swa, w_o_mem,
           w_out, g_ffn, w_up, conv_w, conv_b, w_down):
    weights = (g_mix, g_mem, w_in, q_a_norm, w_q_b, kv_a_norm, w_kv_b, g_q_mla, g_k_mla, g_q_swa, g_k_swa,
               swa_sink, w_mem_kv, g_q_mem, g_k_mem, w_o_mla, w_o_swa, w_o_mem, w_out, g_ffn, w_up, conv_w,
               conv_b, w_down)
    depth = g_mix.shape[0]
    y_prompt, y_sample = x_prompt, x_sample
    tabs = _rope_tables(max(x_prompt.shape[1], x_sample.shape[1]))
    for layer in range(depth):
        w = _prep_weights(*(p[layer] for p in weights))
        y_prompt = _layer(y_prompt, mem_prompt, w, tabs)
        y_sample = _layer(y_sample, mem_sample, w, tabs)
    return (y_prompt, y_sample)
```

```python
import functools
import math

import jax
import jax.numpy as jnp
from jax import lax
from jax.experimental import pallas as pl
from jax.experimental.pallas import tpu as pltpu

D_MODEL = 1024
N_MEM = 256
EPS = 1e-6
ROPE_THETA = 10000.0
MLA_HEADS = 8
MLA_NOPE = 64
MLA_ROPE = 32
MLA_V = 64
MLA_QK = MLA_NOPE + MLA_ROPE
Q_LORA = 384
KV_LORA = 256
SWA_HEADS = 8
SWA_KV_HEADS = 2
SWA_GROUP = SWA_HEADS // SWA_KV_HEADS
SWA_HD = 64
WINDOW = 128
BLOCK = 128
MEM_HEADS = 4
MEM_HD = 128
N_BRANCH = 3
D_FF = 2816
SPLITS = (Q_LORA, KV_LORA, MLA_ROPE, SWA_HEADS * SWA_HD, SWA_KV_HEADS * SWA_HD,
          SWA_KV_HEADS * SWA_HD, MEM_HEADS * MEM_HD, N_BRANCH * D_MODEL)

LANE = 128
MLA_HP = 128
MLA_VROWS = 80
LOG2E = math.log2(math.e)
NEG_BIG = -1e30
MLA_LOGIT_BOUND = 80.0
VMEM_LIMIT = 56 * 1024 * 1024

BF16 = jnp.bfloat16
F32 = jnp.float32


def _dot(a, b):
    return jnp.dot(a, b, preferred_element_type=F32)


def _dot_nt(a, b):
    return lax.dot_general(a, b, (((1,), (1,)), ((), ())), preferred_element_type=F32)


def _dot_tn(a, b):
    return lax.dot_general(a, b, (((0,), (0,)), ((), ())), preferred_element_type=F32)


def _const_spec(shape):
    nd = len(shape)
    return pl.BlockSpec(shape, lambda *_: (0,) * nd, pipeline_mode=pl.Buffered(1))


def _params(semantics):
    return pltpu.CompilerParams(dimension_semantics=semantics, vmem_limit_bytes=VMEM_LIMIT)


def _proj_kernel(x_ref, gmix_ref, wa_ref, wbt_ref, qan_ref, wqt_ref, kvan_ref, wk_ref, wvt_ref,
                 gq_mla_ref, gk_mla_ref, gq_swa_ref, gk_swa_ref, gq_mem_ref,
                 ct16_ref, st16_ref, ck_ref, sak_ref, sbk_ref,
                 ct32_ref, st32_ref, cs_ref, sas_ref, sbs_ref,
                 qt_mla_ref, k_mla_ref, vt_mla_ref, qt_swa_ref, k_swa_ref, vt_swa_ref, q_mem_ref):
    x = x_ref[...]
    tm = x.shape[0]
    h = x * lax.rsqrt(jnp.mean(x * x, axis=-1, keepdims=True) + EPS) * gmix_ref[...]
    hb = h.astype(BF16)
    z = _dot(hb, wa_ref[...])
    c_q = z[:, 0:384]
    c_kv = z[:, 384:640]
    k_s = z[:, 640:768]
    q_m = z[:, 768:1280]
    k_rope = z[:, 1280:1408]

    bt = _dot_nt(wbt_ref[...], hb)
    c_qn = (c_q * lax.rsqrt(jnp.mean(c_q * c_q, axis=-1, keepdims=True) + EPS) * qan_ref[...]).astype(BF16)
    c_kvn = (c_kv * lax.rsqrt(jnp.mean(c_kv * c_kv, axis=-1, keepdims=True) + EPS) * kvan_ref[...]).astype(BF16)
    qt = _dot_nt(wqt_ref[...], c_qn)
    kpre = _dot(c_kvn, wk_ref[...])
    vt = _dot_nt(wvt_ref[...], c_kvn)

    ct = ct16_ref[...]
    st = st16_ref[...]
    gq = gq_mla_ref[...]
    for hd in range(MLA_HEADS):
        blk = qt[hd * MLA_HP:(hd + 1) * MLA_HP]
        ss = jnp.sum(blk * blk, axis=0, keepdims=True)
        y = blk * lax.rsqrt(ss * (1.0 / MLA_QK) + EPS) * gq
        x1 = y[64:80]
        x2 = y[80:96]
        out = jnp.concatenate([y[0:64], x1 * ct - x2 * st, x2 * ct + x1 * st, y[96:128]], axis=0)
        qt_mla_ref[hd * MLA_HP:(hd + 1) * MLA_HP, :] = out.astype(BF16)

    gk = gk_mla_ref[...]
    ss_rope = jnp.sum(k_rope * k_rope, axis=-1, keepdims=True)
    yr = k_rope * gk
    k_roped = yr * ck_ref[...] + pltpu.roll(yr, 112, 1) * sak_ref[...] + pltpu.roll(yr, 16, 1) * sbk_ref[...]
    for hd in range(MLA_HEADS):
        blk = kpre[:, hd * MLA_HP:(hd + 1) * MLA_HP]
        ss = jnp.sum(blk * blk, axis=-1, keepdims=True) + ss_rope
        out = (blk * gk + k_roped) * lax.rsqrt(ss * (1.0 / MLA_QK) + EPS)
        k_mla_ref[hd] = out.astype(BF16)
    ones = jnp.ones((MLA_VROWS - MLA_V, tm), BF16)
    for hd in range(MLA_HEADS):
        vt_mla_ref[hd * MLA_VROWS:hd * MLA_VROWS + MLA_V, :] = vt[hd * MLA_V:(hd + 1) * MLA_V].astype(BF16)
        vt_mla_ref[hd * MLA_VROWS + MLA_V:(hd + 1) * MLA_VROWS, :] = ones

    c32 = ct32_ref[...]
    s32 = st32_ref[...]
    gqs = gq_swa_ref[...]
    for hd in range(SWA_HEADS):
        blk = bt[hd * SWA_HD:(hd + 1) * SWA_HD]
        ss = jnp.sum(blk * blk, axis=0, keepdims=True)
        y = blk * lax.rsqrt(ss * (1.0 / SWA_HD) + EPS) * gqs
        x1 = y[0:32]
        x2 = y[32:64]
        out = jnp.concatenate([x1 * c32 - x2 * s32, x2 * c32 + x1 * s32], axis=0)
        qt_swa_ref[hd * SWA_HD:(hd + 1) * SWA_HD, :] = out.astype(BF16)
    vt_swa_ref[...] = bt[SWA_HEADS * SWA_HD:].astype(BF16)
    lane = lax.broadcasted_iota(jnp.int32, (tm, LANE), 1)
    lo = lane < SWA_HD
    sq = k_s * k_s
    ss_lo = jnp.sum(jnp.where(lo, sq, 0.0), axis=-1, keepdims=True)
    ss_hi = jnp.sum(jnp.where(lo, 0.0, sq), axis=-1, keepdims=True)
    rstd = jnp.where(lo, lax.rsqrt(ss_lo * (1.0 / SWA_HD) + EPS), lax.rsqrt(ss_hi * (1.0 / SWA_HD) + EPS))
    y = k_s * rstd * gk_swa_ref[...]
    out = y * cs_ref[...] + pltpu.roll(y, 96, 1) * sas_ref[...] + pltpu.roll(y, 32, 1) * sbs_ref[...]
    k_swa_ref[...] = out.astype(BF16)

    gqm = gq_mem_ref[...]
    for hd in range(MEM_HEADS):
        blk = q_m[:, hd * MEM_HD:(hd + 1) * MEM_HD]
        ss = jnp.sum(blk * blk, axis=-1, keepdims=True)
        y = blk * lax.rsqrt(ss * (1.0 / MEM_HD) + EPS) * gqm
        q_mem_ref[:, hd * MEM_HD:(hd + 1) * MEM_HD] = y.astype(BF16)


def _proj(x, w, tabs, tm):
    b, s, d = x.shape
    grid = (s // tm, b)
    tok = lambda width: pl.BlockSpec((None, tm, width), lambda j, i: (i, j, 0))
    tok_t = lambda rows: pl.BlockSpec((None, rows, tm), lambda j, i: (i, 0, j))
    tab = lambda width: pl.BlockSpec((tm, width), lambda j, i: (j, 0))
    tab_t = lambda rows: pl.BlockSpec((rows, tm), lambda j, i: (0, j))
    consts = [w['g_mix'], w['wa'], w['wbt'], w['q_a_norm'], w['wqt'], w['kv_a_norm'], w['wk'], w['wvt'],
              w['gq_mla'], w['gk_mla'], w['gq_swa'], w['gk_swa'], w['gq_mem']]
    in_specs = [tok(d)] + [_const_spec(c.shape) for c in consts] + [
        tab_t(16), tab_t(16), tab(LANE), tab(LANE), tab(LANE),
        tab_t(32), tab_t(32), tab(LANE), tab(LANE), tab(LANE)]
    out_shape = (
        jax.ShapeDtypeStruct((b, MLA_HEADS * MLA_HP, s), BF16),
        jax.ShapeDtypeStruct((b, MLA_HEADS, s, MLA_HP), BF16),
        jax.ShapeDtypeStruct((b, MLA_HEADS * MLA_VROWS, s), BF16),
        jax.ShapeDtypeStruct((b, SWA_HEADS * SWA_HD, s), BF16),
        jax.ShapeDtypeStruct((b, s, SWA_KV_HEADS * SWA_HD), BF16),
        jax.ShapeDtypeStruct((b, SWA_KV_HEADS * SWA_HD, s), BF16),
        jax.ShapeDtypeStruct((b, s, MEM_HEADS * MEM_HD), BF16),
    )
    k_heads = pl.BlockSpec((None, MLA_HEADS, tm, MLA_HP), lambda j, i: (i, 0, j, 0))
    out_specs = (tok_t(MLA_HEADS * MLA_HP), k_heads, tok_t(MLA_HEADS * MLA_VROWS),
                 tok_t(SWA_HEADS * SWA_HD), tok(SWA_KV_HEADS * SWA_HD), tok_t(SWA_KV_HEADS * SWA_HD),
                 tok(MEM_HEADS * MEM_HD))
    return pl.pallas_call(
        _proj_kernel, grid=grid, in_specs=in_specs, out_specs=out_specs, out_shape=out_shape,
        compiler_params=_params(("parallel", "parallel")), name="proj",
    )(x, *consts, tabs['ct16'], tabs['st16'], tabs['ck'], tabs['sak'], tabs['sbk'],
      tabs['ct32'], tabs['st32'], tabs['cs'], tabs['sas'], tabs['sbs'])


def _mem_kv_kernel(mem_ref, gmem_ref, wkt_ref, wv_ref, gk_ref, kt_ref, v_ref):
    m = mem_ref[...]
    mn = (m * lax.rsqrt(jnp.mean(m * m, axis=-1, keepdims=True) + EPS) * gmem_ref[...]).astype(BF16)
    kt = _dot_nt(wkt_ref[...], mn)
    gk = gk_ref[...]
    for hd in range(MEM_HEADS):
        blk = kt[hd * MEM_HD:(hd + 1) * MEM_HD]
        ss = jnp.sum(blk * blk, axis=0, keepdims=True)
        kt_ref[hd * MEM_HD:(hd + 1) * MEM_HD, :] = (blk * lax.rsqrt(ss * (1.0 / MEM_HD) + EPS) * gk).astype(BF16)
    v_ref[...] = _dot(mn, wv_ref[...]).astype(BF16)


def _mem_kv(mem, w):
    b, n, d = mem.shape
    hw = MEM_HEADS * MEM_HD
    consts = [w['g_mem'], w['wmkt'], w['wmv'], w['gk_mem']]
    return pl.pallas_call(
        _mem_kv_kernel, grid=(b,),
        in_specs=[pl.BlockSpec((None, n, d), lambda i: (i, 0, 0))] + [_const_spec(c.shape) for c in consts],
        out_specs=(pl.BlockSpec((None, hw, n), lambda i: (i, 0, 0)), pl.BlockSpec((None, n, hw), lambda i: (i, 0, 0))),
        out_shape=(jax.ShapeDtypeStruct((b, hw, n), BF16), jax.ShapeDtypeStruct((b, n, hw), BF16)),
        compiler_params=_params(("parallel",)), name="mem_kv",
    )(mem, *consts)


def _aligned(start, align):
    return start if isinstance(start, int) else pl.multiple_of(start, align)


def _mla_kernel(qt_ref, k_ref, vt_ref, ot_ref, *scratch, tile_fn, tq):
    def body(i, carry):
        cols = pl.ds(pl.multiple_of(i * tq, tq), tq)
        ot_ref[:, cols] = tile_fn(qt_ref[:, cols], k_ref, vt_ref, *scratch)
        return carry

    lax.fori_loop(0, qt_ref.shape[1] // tq, body, 0)


def _mla_online_tile(qt, k_ref, vt_ref, s0_ref, s1_ref, p0_ref, p1_ref, *, tk, sub):
    s = k_ref.shape[0]
    tq = qt.shape[1]
    n = s // tk
    s_refs = (s0_ref, s1_ref)
    p_refs = (p0_ref, p1_ref)

    def step(c, par, carry, logits=True, accum=True, exps=True):
        mc, m, alpha, acc = carry
        if exps:
            m_new = jnp.maximum(m, mc)
            alpha_new = jnp.exp2(m - m_new)
        if accum:
            acc = alpha * acc
        mc_next = None
        for j in range(tk // sub):
            blk = slice(j * sub, (j + 1) * sub)
            if logits:
                off = _aligned((c + 1) * tk + j * sub, sub)
                st = _dot(k_ref[pl.ds(off, sub), :], qt)
                s_refs[1 - par][blk, :] = st
                mj = jnp.max(st, axis=0, keepdims=True)
                mc_next = mj if mc_next is None else jnp.maximum(mc_next, mj)
            if accum:
                off = _aligned((c - 1) * tk + j * sub, sub)
                acc = acc + _dot(vt_ref[:, pl.ds(off, sub)], p_refs[1 - par][blk, :])
            if exps:
                p_refs[par][blk, :] = jnp.exp2((s_refs[par][blk, :] - m_new).astype(BF16))
        return (mc_next if logits else mc, m_new if exps else m, alpha_new if exps else alpha, acc)

    def pair(i, carry):
        return step(2 * i + 2, 0, step(2 * i + 1, 1, carry))

    init = jnp.full((1, tq), NEG_BIG, F32)
    carry = (init, init, jnp.zeros((1, tq), F32), jnp.zeros((MLA_VROWS, tq), F32))
    carry = step(-1, 1, carry, accum=False, exps=False)
    carry = step(0, 0, carry, accum=False)
    carry = lax.fori_loop(0, n // 2 - 1, pair, carry, unroll=True)
    carry = step(n - 1, 1, carry, logits=False)
    _, _, _, acc = step(n, 0, carry, logits=False, exps=False)
    return (acc[0:MLA_V] / acc[MLA_V:MLA_V + 1]).astype(BF16)


def _mla_bounded_tile(qt, k_ref, vt_ref, *, sub):
    s = k_ref.shape[0]
    acc = jnp.zeros((MLA_VROWS, qt.shape[1]), F32)
    for j in range(s // sub):
        blk = slice(j * sub, (j + 1) * sub)
        pt = jnp.exp2(_dot(k_ref[blk, :], qt)).astype(BF16)
        acc = acc + _dot(vt_ref[:, blk], pt)
    return (acc[0:MLA_V] / acc[MLA_V:MLA_V + 1]).astype(BF16)


def _mla_attn(qt, k, vt, tq, tk, bounded):
    b, _, s = qt.shape
    tq_step = min(s, 4 * tq)
    grid = (b, MLA_HEADS, s // tq_step)
    assert (s // tk) % 2 == 0 and s % tq_step == 0
    if bounded:
        tile_fn = functools.partial(_mla_bounded_tile, sub=min(s, 2048))
        scratch = []
    else:
        tile_fn = functools.partial(_mla_online_tile, tk=tk, sub=min(tk, 256))
        scratch = [pltpu.VMEM((tk, tq), F32), pltpu.VMEM((tk, tq), F32),
                   pltpu.VMEM((tk, tq), BF16), pltpu.VMEM((tk, tq), BF16)]
    return pl.pallas_call(
        functools.partial(_mla_kernel, tile_fn=tile_fn, tq=tq), grid=grid,
        in_specs=[pl.BlockSpec((None, MLA_HP, tq_step), lambda i, h, j: (i, h, j)),
                  pl.BlockSpec((None, None, s, MLA_HP), lambda i, h, j: (i, h, 0, 0)),
                  pl.BlockSpec((None, MLA_VROWS, s), lambda i, h, j: (i, h, 0))],
        out_specs=pl.BlockSpec((None, MLA_V, tq_step), lambda i, h, j: (i, h, j)),
        out_shape=jax.ShapeDtypeStruct((b, MLA_HEADS * MLA_V, s), BF16),
        scratch_shapes=scratch,
        compiler_params=_params(("parallel", "parallel", "arbitrary")),
        name="mla_attn_bounded" if bounded else "mla_attn",
    )(qt, k, vt)


SWA_SPAN = 3 * BLOCK


def _swa_bias():
    r = jnp.arange(SWA_SPAN)[:, None]
    c = jnp.arange(BLOCK)[None, :]
    return jnp.stack([jnp.where(jnp.abs(lead * BLOCK + c - r) <= WINDOW, 0.0, NEG_BIG) for lead in range(3)]).astype(F32)


def _swa_kernel(qt_ref, k_ref, vt_ref, sink_ref, bias_ref, ot_ref):
    s = k_ref.shape[0]
    t = qt_ref.shape[1]
    nb = s // BLOCK
    gw = SWA_GROUP * BLOCK
    zeros = jnp.zeros((SWA_HD, gw), BF16)
    ones = jnp.ones((16, SWA_SPAN), BF16)
    logits = []
    for blk in range(t // BLOCK):
        n = pl.program_id(1) * (t // BLOCK) + blk
        start = pl.multiple_of(jnp.clip((n - 1) * BLOCK, 0, s - SWA_SPAN), BLOCK)
        kwin = k_ref[pl.ds(start, SWA_SPAN), :]
        vwin = vt_ref[:, pl.ds(start, SWA_SPAN)]
        lead = jnp.where(n == 0, 0, jnp.where(n == nb - 1, 2, 1))
        bias = bias_ref[lead]
        bias = jnp.concatenate([bias] * SWA_GROUP, axis=1)
        for g in range(SWA_KV_HEADS):
            qg = jnp.concatenate(
                [qt_ref[(g * SWA_GROUP + j) * SWA_HD:(g * SWA_GROUP + j + 1) * SWA_HD, blk * BLOCK:(blk + 1) * BLOCK]
                 for j in range(SWA_GROUP)], axis=1)
            qpad = jnp.concatenate([qg, zeros] if g == 0 else [zeros, qg], axis=0)
            st = _dot(kwin, qpad) + bias
            vext = jnp.concatenate([vwin[g * SWA_HD:(g + 1) * SWA_HD], ones], axis=0)
            logits.append((blk, g, st, vext))
    weights = []
    for blk, g, st, vext in logits:
        sk = sink_ref[g:g + 1, :]
        m = jnp.maximum(jnp.max(st, axis=0, keepdims=True), sk)
        weights.append((blk, g, jnp.exp2((st - m).astype(BF16)), jnp.exp2(sk - m), vext))
    for blk, g, p, p_sink, vext in weights:
        acc = _dot(vext, p)
        ot = acc[0:SWA_HD] / (acc[SWA_HD:SWA_HD + 1] + p_sink)
        for j in range(SWA_GROUP):
            hd = g * SWA_GROUP + j
            ot_ref[hd * SWA_HD:(hd + 1) * SWA_HD, blk * BLOCK:(blk + 1) * BLOCK] = (
                ot[:, j * BLOCK:(j + 1) * BLOCK].astype(BF16))


def _swa_attn(qt, k, vt, sink_rows, t):
    b, hw, s = qt.shape
    kvw = SWA_KV_HEADS * SWA_HD
    bias = _swa_bias()
    return pl.pallas_call(
        _swa_kernel, grid=(b, s // t),
        in_specs=[pl.BlockSpec((None, hw, t), lambda i, j: (i, 0, j)),
                  pl.BlockSpec((None, s, kvw), lambda i, j: (i, 0, 0)),
                  pl.BlockSpec((None, kvw, s), lambda i, j: (i, 0, 0)),
                  _const_spec(sink_rows.shape), _const_spec(bias.shape)],
        out_specs=pl.BlockSpec((None, hw, t), lambda i, j: (i, 0, j)),
        out_shape=jax.ShapeDtypeStruct((b, hw, s), BF16),
        compiler_params=_params(("parallel", "arbitrary")), name="swa_attn",
    )(qt, k, vt, sink_rows, bias)


def _sigmoid(v):
    return 1.0 / (1.0 + jnp.exp(-v))


def _merge_kernel(x_ref, ot_mla_ref, ot_swa_ref, q_mem_ref, kt_mem_ref, v_mem_ref, gmix_ref, wg_ref,
                  wo_mla_ref, wo_swa_ref, wo_mem_ref, wout_ref, y_ref):
    x = x_ref[...]
    hb = (x * lax.rsqrt(jnp.mean(x * x, axis=-1, keepdims=True) + EPS) * gmix_ref[...]).astype(BF16)

    heads = [slice(hd * MEM_HD, (hd + 1) * MEM_HD) for hd in range(MEM_HEADS)]
    logits = [_dot(q_mem_ref[:, hs], kt_mem_ref[hs, :]) for hs in heads]
    probs = [jnp.exp2(sc - jnp.max(sc, axis=-1, keepdims=True)) for sc in logits]
    o_heads = [_dot(p.astype(BF16), v_mem_ref[:, hs]) / jnp.sum(p, axis=-1, keepdims=True)
               for p, hs in zip(probs, heads)]
    o_mem = jnp.concatenate(o_heads, axis=1).astype(BF16)

    merged = _sigmoid(_dot(hb, wg_ref[:, 0:D_MODEL])) * _dot_tn(ot_mla_ref[...], wo_mla_ref[...])
    merged += _sigmoid(_dot(hb, wg_ref[:, D_MODEL:2 * D_MODEL])) * _dot_tn(ot_swa_ref[...], wo_swa_ref[...])
    merged += _sigmoid(_dot(hb, wg_ref[:, 2 * D_MODEL:3 * D_MODEL])) * _dot(o_mem, wo_mem_ref[...])
    y_ref[...] = x + _dot(merged.astype(BF16), wout_ref[...])


def _merge(x, ot_mla, ot_swa, q_mem, kt_mem, v_mem, w, tm):
    b, s, d = x.shape
    n_mem = v_mem.shape[1]
    hw = MEM_HEADS * MEM_HD
    tok = lambda width: pl.BlockSpec((None, tm, width), lambda i, j: (i, j, 0))
    tok_t = lambda rows: pl.BlockSpec((None, rows, tm), lambda i, j: (i, 0, j))
    consts = [w['g_mix'], w['wg'], w['wo_mla'], w['wo_swa'], w['wo_mem'], w['wout']]
    return pl.pallas_call(
        _merge_kernel, grid=(b, s // tm),
        in_specs=[tok(d), tok_t(MLA_HEADS * MLA_V), tok_t(SWA_HEADS * SWA_HD), tok(hw),
                  pl.BlockSpec((None, hw, n_mem), lambda i, j: (i, 0, 0)),
                  pl.BlockSpec((None, n_mem, hw), lambda i, j: (i, 0, 0))] + [_const_spec(c.shape) for c in consts],
        out_specs=tok(d), out_shape=jax.ShapeDtypeStruct((b, s, d), F32),
        compiler_params=_params(("parallel", "parallel")), name="merge",
    )(x, ot_mla, ot_swa, q_mem, kt_mem, v_mem, *consts)


HALO = 8


def _ffn_kernel(x_ref, prev_ref, next_ref, gffn_ref, wup_ref, cw_ref, cb_ref, wdown_ref, y_ref):
    j = pl.program_id(1)
    nj = pl.num_programs(1)
    x = x_ref[...]
    tm = x.shape[0]
    g = gffn_ref[...]

    def norm(v):
        return v * lax.rsqrt(jnp.mean(v * v, axis=-1, keepdims=True) + EPS) * g

    hp = jnp.where(j > 0, norm(prev_ref[...]), 0.0)
    hn = jnp.where(j < nj - 1, norm(next_ref[...]), 0.0)
    hext = jnp.concatenate([hp, norm(x), hn], axis=0).astype(BF16)
    u = _dot(hext, wup_ref[...])
    cw = cw_ref[...]
    conv = (u[HALO - 1:HALO - 1 + tm] * cw[0:1] + u[HALO:HALO + tm] * cw[1:2]
            + u[HALO + 1:HALO + 1 + tm] * cw[2:3] + cb_ref[...])
    a = conv[:, :D_FF]
    val = conv[:, D_FF:]
    act = (a * _sigmoid(a) * val).astype(BF16)
    y_ref[...] = x + _dot(act, wdown_ref[...])


def _ffn(x, w, tm):
    b, s, d = x.shape
    nh = tm // HALO
    last = s // HALO - 1
    consts = [w['g_ffn'], w['wup'], w['conv_w'], w['conv_b'], w['wdown']]
    return pl.pallas_call(
        _ffn_kernel, grid=(b, s // tm),
        in_specs=[pl.BlockSpec((None, tm, d), lambda i, j: (i, j, 0)),
                  pl.BlockSpec((None, HALO, d), lambda i, j: (i, jnp.maximum(j * nh - 1, 0), 0)),
                  pl.BlockSpec((None, HALO, d), lambda i, j: (i, jnp.minimum((j + 1) * nh, last), 0))]
                 + [_const_spec(c.shape) for c in consts],
        out_specs=pl.BlockSpec((None, tm, d), lambda i, j: (i, j, 0)),
        out_shape=jax.ShapeDtypeStruct((b, s, d), F32),
        compiler_params=_params(("parallel", "parallel")), name="ffn",
    )(x, x, x, *consts)


def _prep_weights(g_mix, g_mem, w_in, q_a_norm, w_q_b, kv_a_norm, w_kv_b, g_q_mla, g_k_mla, g_q_swa, g_k_swa,
                  swa_sink, w_mem_kv, g_q_mem, g_k_mem, w_o_mla, w_o_swa, w_o_mem, w_out, g_ffn, w_up, conv_w,
                  conv_b, w_down):
    offs = [0]
    for sp in SPLITS:
        offs.append(offs[-1] + sp)
    w_cq, w_ckv, w_kr, w_qs, w_ks, w_vs, w_qm, w_gate = (w_in[:, offs[i]:offs[i + 1]] for i in range(8))
    row = lambda v: v.reshape(1, -1).astype(F32)
    col = lambda v: v.reshape(-1, 1).astype(F32)
    w = {}
    w['g_mix'] = row(g_mix)
    w['g_mem'] = row(g_mem)
    w['g_ffn'] = row(g_ffn)
    w['q_a_norm'] = row(q_a_norm)
    w['kv_a_norm'] = row(kv_a_norm)
    w_kr_placed = jnp.pad(w_kr, ((0, 0), (MLA_NOPE, MLA_HP - MLA_QK)))
    w['wa'] = jnp.concatenate([w_cq, w_ckv, w_ks, w_qm, w_kr_placed], axis=1).astype(BF16)
    w['wbt'] = jnp.concatenate([w_qs, w_vs], axis=1).T.astype(BF16)
    w['wg'] = w_gate.astype(BF16)
    wq = jnp.pad(w_q_b.reshape(Q_LORA, MLA_HEADS, MLA_QK), ((0, 0), (0, 0), (0, MLA_HP - MLA_QK)))
    w['wqt'] = wq.reshape(Q_LORA, MLA_HEADS * MLA_HP).T.astype(BF16)
    wkv = w_kv_b.reshape(KV_LORA, MLA_HEADS, MLA_NOPE + MLA_V)
    wk_nope = jnp.pad(wkv[:, :, :MLA_NOPE], ((0, 0), (0, 0), (0, MLA_HP - MLA_NOPE))).reshape(KV_LORA, -1)
    w['wk'] = wk_nope.astype(BF16)
    w['wvt'] = wkv[:, :, MLA_NOPE:].reshape(KV_LORA, MLA_HEADS * MLA_V).T.astype(BF16)
    sc_mla = MLA_QK ** -0.5 * LOG2E
    w['gq_mla'] = col(jnp.pad(g_q_mla * sc_mla, (0, MLA_HP - MLA_QK)))
    w['gk_mla'] = row(jnp.pad(g_k_mla, (0, MLA_HP - MLA_QK)))
    w['mla_logit_bound'] = 1.02 * MLA_QK * jnp.max(jnp.abs(w['gq_mla'])) * jnp.max(jnp.abs(w['gk_mla']))
    w['gq_swa'] = col(g_q_swa * (SWA_HD ** -0.5 * LOG2E))
    w['gk_swa'] = row(jnp.tile(g_k_swa, SWA_KV_HEADS))
    w['gq_mem'] = row(g_q_mem * (MEM_HD ** -0.5 * LOG2E))
    w['gk_mem'] = col(g_k_mem)
    w['sink_rows'] = jnp.repeat(swa_sink.astype(F32) * LOG2E, BLOCK).reshape(SWA_KV_HEADS, SWA_GROUP * BLOCK)
    hw = MEM_HEADS * MEM_HD
    w['wmkt'] = w_mem_kv[:, :hw].T.astype(BF16)
    w['wmv'] = w_mem_kv[:, hw:].astype(BF16)
    w['wo_mla'] = w_o_mla.astype(BF16)
    w['wo_swa'] = w_o_swa.astype(BF16)
    w['wo_mem'] = w_o_mem.astype(BF16)
    w['wout'] = w_out.astype(BF16)
    w['wup'] = w_up.astype(BF16)
    w['wdown'] = w_down.astype(BF16)
    w['conv_w'] = conv_w.astype(F32)
    w['conv_b'] = row(conv_b)
    return w


def _rope_tables(s):
    def cs(dim):
        inv = 1.0 / (ROPE_THETA ** (jnp.arange(0, dim, 2, dtype=F32) / dim))
        ang = jnp.arange(s, dtype=F32)[:, None] * inv[None, :]
        return jnp.cos(ang), jnp.sin(ang)

    c16, s16 = cs(MLA_ROPE)
    c32, s32 = cs(SWA_HD)
    z = lambda n: jnp.zeros((s, n), F32)
    t = {'ct16': c16.T, 'st16': s16.T, 'ct32': c32.T, 'st32': s32.T}
    t['ck'] = jnp.concatenate([jnp.ones((s, MLA_NOPE), F32), c16, c16, z(MLA_HP - MLA_QK)], axis=1)
    t['sak'] = jnp.concatenate([z(MLA_NOPE), -s16, z(16), z(MLA_HP - MLA_QK)], axis=1)
    t['sbk'] = jnp.concatenate([z(MLA_NOPE), z(16), s16, z(MLA_HP - MLA_QK)], axis=1)
    t['cs'] = jnp.concatenate([c32, c32] * SWA_KV_HEADS, axis=1)
    t['sas'] = jnp.concatenate([-s32, z(32)] * SWA_KV_HEADS, axis=1)
    t['sbs'] = jnp.concatenate([z(32), s32] * SWA_KV_HEADS, axis=1)
    return t


def _tiles(s):
    pick = lambda want: want if s % want == 0 else BLOCK
    mla_k = 1024 if s % 2048 == 0 else BLOCK
    return dict(proj=pick(256), mla_q=pick(256), mla_q_bounded=pick(512), mla_k=mla_k, swa=pick(512),
                merge=pick(512), ffn=pick(512))


def _layer(x, mem, w, tabs):
    b, s, d = x.shape
    assert d == D_MODEL and s % BLOCK == 0 and s >= 3 * BLOCK
    ts = _tiles(s)
    qt_mla, k_mla, vt_mla, qt_swa, k_swa, vt_swa, q_mem = _proj(x, w, tabs, ts['proj'])
    kt_mem, v_mem = _mem_kv(mem, w)
    ot_mla = lax.cond(
        w['mla_logit_bound'] <= MLA_LOGIT_BOUND,
        lambda: _mla_attn(qt_mla, k_mla, vt_mla, ts['mla_q_bounded'], ts['mla_k'], True),
        lambda: _mla_attn(qt_mla, k_mla, vt_mla, ts['mla_q'], ts['mla_k'], False))
    ot_swa = _swa_attn(qt_swa, k_swa, vt_swa, w['sink_rows'], ts['swa'])
    x1 = _merge(x, ot_mla, ot_swa, q_mem, kt_mem, v_mem, w, ts['merge'])
    return _ffn(x1, w, ts['ffn'])


def kernel(x_prompt, x_sample, mem_prompt, mem_sample, g_mix, g_mem, w_in, q_a_norm, w_q_b, kv_a_norm, w_kv_b,
           g_q_mla, g_k_mla, g_q_swa, g_k_swa, swa_sink, w_mem_kv, g_q_mem, g_k_mem, w_o_mla, w_o_swa, w_o_mem,
           w_out, g_ffn, w_up, conv_w, conv_b, w_down):
    weights = (g_mix, g_mem, w_in, q_a_norm, w_q_b, kv_a_norm, w_kv_b, g_q_mla, g_k_mla, g_q_swa, g_k_swa,
               swa_sink, w_mem_kv, g_q_mem, g_k_mem, w_o_mla, w_o_swa, w_o_mem, w_out, g_ffn, w_up, conv_w,
               conv_b, w_down)
    depth = g_mix.shape[0]
    y_prompt, y_sample = x_prompt, x_sample
    tabs = _rope_tables(max(x_prompt.shape[1], x_sample.shape[1]))
    for layer in range(depth):
        w = _prep_weights(*(p[layer] for p in weights))
        y_prompt = _layer(y_prompt, mem_prompt, w, tabs)
        y_sample = _layer(y_sample, mem_sample, w, tabs)
    return (y_prompt, y_sample)
```

```python
import functools
import math

import jax
import jax.numpy as jnp
from jax import lax
from jax.experimental import pallas as pl
from jax.experimental.pallas import tpu as pltpu

D_MODEL = 1024
N_MEM = 256
EPS = 1e-6
ROPE_THETA = 10000.0
MLA_HEADS = 8
MLA_NOPE = 64
MLA_ROPE = 32
MLA_V = 64
MLA_QK = MLA_NOPE + MLA_ROPE
Q_LORA = 384
KV_LORA = 256
SWA_HEADS = 8
SWA_KV_HEADS = 2
SWA_GROUP = SWA_HEADS // SWA_KV_HEADS
SWA_HD = 64
WINDOW = 128
BLOCK = 128
MEM_HEADS = 4
MEM_HD = 128
N_BRANCH = 3
D_FF = 2816
SPLITS = (Q_LORA, KV_LORA, MLA_ROPE, SWA_HEADS * SWA_HD, SWA_KV_HEADS * SWA_HD,
          SWA_KV_HEADS * SWA_HD, MEM_HEADS * MEM_HD, N_BRANCH * D_MODEL)

LANE = 128
MLA_HP = 128
MLA_VROWS = 80
LOG2E = math.log2(math.e)
NEG_BIG = -1e30
MLA_LOGIT_BOUND = 80.0
VMEM_LIMIT = 56 * 1024 * 1024

BF16 = jnp.bfloat16
F32 = jnp.float32


def _dot(a, b):
    return jnp.dot(a, b, preferred_element_type=F32)


def _dot_nt(a, b):
    return lax.dot_general(a, b, (((1,), (1,)), ((), ())), preferred_element_type=F32)


def _dot_tn(a, b):
    return lax.dot_general(a, b, (((0,), (0,)), ((), ())), preferred_element_type=F32)


def _const_spec(shape):
    nd = len(shape)
    return pl.BlockSpec(shape, lambda *_: (0,) * nd, pipeline_mode=pl.Buffered(1))


def _params(semantics):
    return pltpu.CompilerParams(dimension_semantics=semantics, vmem_limit_bytes=VMEM_LIMIT)


def _proj_kernel(x_ref, gmix_ref, wa_ref, wbt_ref, qan_ref, wqt_ref, kvan_ref, wk_ref, wvt_ref,
                 gq_mla_ref, gk_mla_ref, gq_swa_ref, gk_swa_ref, gq_mem_ref,
                 ct16_ref, st16_ref, ck_ref, sak_ref, sbk_ref,
                 ct32_ref, st32_ref, cs_ref, sas_ref, sbs_ref,
                 qt_mla_ref, k_mla_ref, vt_mla_ref, qt_swa_ref, k_swa_ref, vt_swa_ref, q_mem_ref):
    x = x_ref[...]
    tm = x.shape[0]
    h = x * lax.rsqrt(jnp.mean(x * x, axis=-1, keepdims=True) + EPS) * gmix_ref[...]
    hb = h.astype(BF16)
    z = _dot(hb, wa_ref[...])
    c_q = z[:, 0:384]
    c_kv = z[:, 384:640]
    k_s = z[:, 640:768]
    q_m = z[:, 768:1280]
    k_rope = z[:, 1280:1408]

    bt = _dot_nt(wbt_ref[...], hb)
    c_qn = (c_q * lax.rsqrt(jnp.mean(c_q * c_q, axis=-1, keepdims=True) + EPS) * qan_ref[...]).astype(BF16)
    c_kvn = (c_kv * lax.rsqrt(jnp.mean(c_kv * c_kv, axis=-1, keepdims=True) + EPS) * kvan_ref[...]).astype(BF16)
    qt = _dot_nt(wqt_ref[...], c_qn)
    kpre = _dot(c_kvn, wk_ref[...])
    vt = _dot_nt(wvt_ref[...], c_kvn)

    ct = ct16_ref[...]
    st = st16_ref[...]
    gq = gq_mla_ref[...]
    for hd in range(MLA_HEADS):
        blk = qt[hd * MLA_HP:(hd + 1) * MLA_HP]
        ss = jnp.sum(blk * blk, axis=0, keepdims=True)
        y = blk * lax.rsqrt(ss * (1.0 / MLA_QK) + EPS) * gq
        x1 = y[64:80]
        x2 = y[80:96]
        out = jnp.concatenate([y[0:64], x1 * ct - x2 * st, x2 * ct + x1 * st, y[96:128]], axis=0)
        qt_mla_ref[hd * MLA_HP:(hd + 1) * MLA_HP, :] = out.astype(BF16)

    gk = gk_mla_ref[...]
    ss_rope = jnp.sum(k_rope * k_rope, axis=-1, keepdims=True)
    yr = k_rope * gk
    k_roped = yr * ck_ref[...] + pltpu.roll(yr, 112, 1) * sak_ref[...] + pltpu.roll(yr, 16, 1) * sbk_ref[...]
    for hd in range(MLA_HEADS):
        blk = kpre[:, hd * MLA_HP:(hd + 1) * MLA_HP]
        ss = jnp.sum(blk * blk, axis=-1, keepdims=True) + ss_rope
        out = (blk * gk + k_roped) * lax.rsqrt(ss * (1.0 / MLA_QK) + EPS)
        k_mla_ref[hd] = out.astype(BF16)
    ones = jnp.ones((MLA_VROWS - MLA_V, tm), BF16)
    for hd in range(MLA_HEADS):
        vt_mla_ref[hd * MLA_VROWS:hd * MLA_VROWS + MLA_V, :] = vt[hd * MLA_V:(hd + 1) * MLA_V].astype(BF16)
        vt_mla_ref[hd * MLA_VROWS + MLA_V:(hd + 1) * MLA_VROWS, :] = ones

    c32 = ct32_ref[...]
    s32 = st32_ref[...]
    gqs = gq_swa_ref[...]
    for hd in range(SWA_HEADS):
        blk = bt[hd * SWA_HD:(hd + 1) * SWA_HD]
        ss = jnp.sum(blk * blk, axis=0, keepdims=True)
        y = blk * lax.rsqrt(ss * (1.0 / SWA_HD) + EPS) * gqs
        x1 = y[0:32]
        x2 = y[32:64]
        out = jnp.concatenate([x1 * c32 - x2 * s32, x2 * c32 + x1 * s32], axis=0)
        qt_swa_ref[hd * SWA_HD:(hd + 1) * SWA_HD, :] = out.astype(BF16)
    vt_swa_ref[...] = bt[SWA_HEADS * SWA_HD:].astype(BF16)
    lane = lax.broadcasted_iota(jnp.int32, (tm, LANE), 1)
    lo = lane < SWA_HD
    sq = k_s * k_s
    ss_lo = jnp.sum(jnp.where(lo, sq, 0.0), axis=-1, keepdims=True)
    ss_hi = jnp.sum(jnp.where(lo, 0.0, sq), axis=-1, keepdims=True)
    rstd = jnp.where(lo, lax.rsqrt(ss_lo * (1.0 / SWA_HD) + EPS), lax.rsqrt(ss_hi * (1.0 / SWA_HD) + EPS))
    y = k_s * rstd * gk_swa_ref[...]
    out = y * cs_ref[...] + pltpu.roll(y, 96, 1) * sas_ref[...] + pltpu.roll(y, 32, 1) * sbs_ref[...]
    k_swa_ref[...] = out.astype(BF16)

    gqm = gq_mem_ref[...]
    for hd in range(MEM_HEADS):
        blk = q_m[:, hd * MEM_HD:(hd + 1) * MEM_HD]
        ss = jnp.sum(blk * blk, axis=-1, keepdims=True)
        y = blk * lax.rsqrt(ss * (1.0 / MEM_HD) + EPS) * gqm
        q_mem_ref[:, hd * MEM_HD:(hd + 1) * MEM_HD] = y.astype(BF16)


def _proj(x, w, tabs, tm):
    b, s, d = x.shape
    grid = (s // tm, b)
    tok = lambda width: pl.BlockSpec((None, tm, width), lambda j, i: (i, j, 0))
    tok_t = lambda rows: pl.BlockSpec((None, rows, tm), lambda j, i: (i, 0, j))
    tab = lambda width: pl.BlockSpec((tm, width), lambda j, i: (j, 0))
    tab_t = lambda rows: pl.BlockSpec((rows, tm), lambda j, i: (0, j))
    consts = [w['g_mix'], w['wa'], w['wbt'], w['q_a_norm'], w['wqt'], w['kv_a_norm'], w['wk'], w['wvt'],
              w['gq_mla'], w['gk_mla'], w['gq_swa'], w['gk_swa'], w['gq_mem']]
    in_specs = [tok(d)] + [_const_spec(c.shape) for c in consts] + [
        tab_t(16), tab_t(16), tab(LANE), tab(LANE), tab(LANE),
        tab_t(32), tab_t(32), tab(LANE), tab(LANE), tab(LANE)]
    out_shape = (
        jax.ShapeDtypeStruct((b, MLA_HEADS * MLA_HP, s), BF16),
        jax.ShapeDtypeStruct((b, MLA_HEADS, s, MLA_HP), BF16),
        jax.ShapeDtypeStruct((b, MLA_HEADS * MLA_VROWS, s), BF16),
        jax.ShapeDtypeStruct((b, SWA_HEADS * SWA_HD, s), BF16),
        jax.ShapeDtypeStruct((b, s, SWA_KV_HEADS * SWA_HD), BF16),
        jax.ShapeDtypeStruct((b, SWA_KV_HEADS * SWA_HD, s), BF16),
        jax.ShapeDtypeStruct((b, s, MEM_HEADS * MEM_HD), BF16),
    )
    k_heads = pl.BlockSpec((None, MLA_HEADS, tm, MLA_HP), lambda j, i: (i, 0, j, 0))
    out_specs = (tok_t(MLA_HEADS * MLA_HP), k_heads, tok_t(MLA_HEADS * MLA_VROWS),
                 tok_t(SWA_HEADS * SWA_HD), tok(SWA_KV_HEADS * SWA_HD), tok_t(SWA_KV_HEADS * SWA_HD),
                 tok(MEM_HEADS * MEM_HD))
    return pl.pallas_call(
        _proj_kernel, grid=grid, in_specs=in_specs, out_specs=out_specs, out_shape=out_shape,
        compiler_params=_params(("parallel", "parallel")), name="proj",
    )(x, *consts, tabs['ct16'], tabs['st16'], tabs['ck'], tabs['sak'], tabs['sbk'],
      tabs['ct32'], tabs['st32'], tabs['cs'], tabs['sas'], tabs['sbs'])


def _mem_kv_kernel(mem_ref, gmem_ref, wkt_ref, wv_ref, gk_ref, kt_ref, v_ref):
    m = mem_ref[...]
    mn = (m * lax.rsqrt(jnp.mean(m * m, axis=-1, keepdims=True) + EPS) * gmem_ref[...]).astype(BF16)
    kt = _dot_nt(wkt_ref[...], mn)
    gk = gk_ref[...]
    for hd in range(MEM_HEADS):
        blk = kt[hd * MEM_HD:(hd + 1) * MEM_HD]
        ss = jnp.sum(blk * blk, axis=0, keepdims=True)
        kt_ref[hd * MEM_HD:(hd + 1) * MEM_HD, :] = (blk * lax.rsqrt(ss * (1.0 / MEM_HD) + EPS) * gk).astype(BF16)
    v_ref[...] = _dot(mn, wv_ref[...]).astype(BF16)


def _mem_kv(mem, w):
    b, n, d = mem.shape
    hw = MEM_HEADS * MEM_HD
    consts = [w['g_mem'], w['wmkt'], w['wmv'], w['gk_mem']]
    return pl.pallas_call(
        _mem_kv_kernel, grid=(b,),
        in_specs=[pl.BlockSpec((None, n, d), lambda i: (i, 0, 0))] + [_const_spec(c.shape) for c in consts],
        out_specs=(pl.BlockSpec((None, hw, n), lambda i: (i, 0, 0)), pl.BlockSpec((None, n, hw), lambda i: (i, 0, 0))),
        out_shape=(jax.ShapeDtypeStruct((b, hw, n), BF16), jax.ShapeDtypeStruct((b, n, hw), BF16)),
        compiler_params=_params(("parallel",)), name="mem_kv",
    )(mem, *consts)


def _aligned(start, align):
    return start if isinstance(start, int) else pl.multiple_of(start, align)


def _mla_kernel(qt_ref, k_ref, vt_ref, ot_ref, *scratch, tile_fn, tq):
    def body(i, carry):
        cols = pl.ds(pl.multiple_of(i * tq, tq), tq)
        ot_ref[:, cols] = tile_fn(qt_ref[:, cols], k_ref, vt_ref, *scratch)
        return carry

    lax.fori_loop(0, qt_ref.shape[1] // tq, body, 0)


def _mla_online_tile(qt, k_ref, vt_ref, s0_ref, s1_ref, p0_ref, p1_ref, *, tk, sub):
    s = k_ref.shape[0]
    tq = qt.shape[1]
    n = s // tk
    s_refs = (s0_ref, s1_ref)
    p_refs = (p0_ref, p1_ref)

    def step(c, par, carry, logits=True, accum=True, exps=True):
        mc, m, alpha, acc = carry
        if exps:
            m_new = jnp.maximum(m, mc)
            alpha_new = jnp.exp2(m - m_new)
        if accum:
            acc = alpha * acc
        mc_next = None
        for j in range(tk // sub):
            blk = slice(j * sub, (j + 1) * sub)
            if logits:
                off = _aligned((c + 1) * tk + j * sub, sub)
                st = _dot(k_ref[pl.ds(off, sub), :], qt)
                s_refs[1 - par][blk, :] = st
                mj = jnp.max(st, axis=0, keepdims=True)
                mc_next = mj if mc_next is None else jnp.maximum(mc_next, mj)
            if accum:
                off = _aligned((c - 1) * tk + j * sub, sub)
                acc = acc + _dot(vt_ref[:, pl.ds(off, sub)], p_refs[1 - par][blk, :])
            if exps:
                p_refs[par][blk, :] = jnp.exp2((s_refs[par][blk, :] - m_new).astype(BF16))
        return (mc_next if logits else mc, m_new if exps else m, alpha_new if exps else alpha, acc)

    def pair(i, carry):
        return step(2 * i + 2, 0, step(2 * i + 1, 1, carry))

    init = jnp.full((1, tq), NEG_BIG, F32)
    carry = (init, init, jnp.zeros((1, tq), F32), jnp.zeros((MLA_VROWS, tq), F32))
    carry = step(-1, 1, carry, accum=False, exps=False)
    carry = step(0, 0, carry, accum=False)
    carry = lax.fori_loop(0, n // 2 - 1, pair, carry, unroll=True)
    carry = step(n - 1, 1, carry, logits=False)
    _, _, _, acc = step(n, 0, carry, logits=False, exps=False)
    return (acc[0:MLA_V] / acc[MLA_V:MLA_V + 1]).astype(BF16)


def _mla_bounded_tile(qt, k_ref, vt_ref, *, sub):
    s = k_ref.shape[0]
    acc = jnp.zeros((MLA_VROWS, qt.shape[1]), F32)
    for j in range(s // sub):
        blk = slice(j * sub, (j + 1) * sub)
        pt = jnp.exp2(_dot(k_ref[blk, :], qt)).astype(BF16)
        acc = acc + _dot(vt_ref[:, blk], pt)
    return (acc[0:MLA_V] / acc[MLA_V:MLA_V + 1]).astype(BF16)


def _mla_attn(qt, k, vt, tq, tk, bounded):
    b, _, s = qt.shape
    tq_step = min(s, 4 * tq)
    grid = (b, MLA_HEADS, s // tq_step)
    assert (s // tk) % 2 == 0 and s % tq_step == 0
    if bounded:
        tile_fn = functools.partial(_mla_bounded_tile, sub=min(s, 2048))
        scratch = []
    else:
        tile_fn = functools.partial(_mla_online_tile, tk=tk, sub=min(tk, 256))
        scratch = [pltpu.VMEM((tk, tq), F32), pltpu.VMEM((tk, tq), F32),
                   pltpu.VMEM((tk, tq), BF16), pltpu.VMEM((tk, tq), BF16)]
    return pl.pallas_call(
        functools.partial(_mla_kernel, tile_fn=tile_fn, tq=tq), grid=grid,
        in_specs=[pl.BlockSpec((None, MLA_HP, tq_step), lambda i, h, j: (i, h, j)),
                  pl.BlockSpec((None, None, s, MLA_HP), lambda i, h, j: (i, h, 0, 0)),
                  pl.BlockSpec((None, MLA_VROWS, s), lambda i, h, j: (i, h, 0))],
        out_specs=pl.BlockSpec((None, MLA_V, tq_step), lambda i, h, j: (i, h, j)),
        out_shape=jax.ShapeDtypeStruct((b, MLA_HEADS * MLA_V, s), BF16),
        scratch_shapes=scratch,
        compiler_params=_params(("parallel", "parallel", "arbitrary")),
        name="mla_attn_bounded" if bounded else "mla_attn",
    )(qt, k, vt)


SWA_SPAN = 3 * BLOCK


def _swa_bias():
    r = jnp.arange(SWA_SPAN)[:, None]
    c = jnp.arange(BLOCK)[None, :]
    return jnp.stack([jnp.where(jnp.abs(lead * BLOCK + c - r) <= WINDOW, 0.0, NEG_BIG) for lead in range(3)]).astype(F32)


def _swa_kernel(qt_ref, k_ref, vt_ref, sink_ref, bias_ref, ot_ref):
    s = k_ref.shape[0]
    t = qt_ref.shape[1]
    nb = s // BLOCK
    gw = SWA_GROUP * BLOCK
    zeros = jnp.zeros((SWA_HD, gw), BF16)
    ones = jnp.ones((16, SWA_SPAN), BF16)
    logits = []
    for blk in range(t // BLOCK):
        n = pl.program_id(1) * (t // BLOCK) + blk
        start = pl.multiple_of(jnp.clip((n - 1) * BLOCK, 0, s - SWA_SPAN), BLOCK)
        kwin = k_ref[pl.ds(start, SWA_SPAN), :]
        vwin = vt_ref[:, pl.ds(start, SWA_SPAN)]
        lead = jnp.where(n == 0, 0, jnp.where(n == nb - 1, 2, 1))
        bias = bias_ref[lead]
        bias = jnp.concatenate([bias] * SWA_GROUP, axis=1)
        for g in range(SWA_KV_HEADS):
            qg = jnp.concatenate(
                [qt_ref[(g * SWA_GROUP + j) * SWA_HD:(g * SWA_GROUP + j + 1) * SWA_HD, blk * BLOCK:(blk + 1) * BLOCK]
                 for j in range(SWA_GROUP)], axis=1)
            qpad = jnp.concatenate([qg, zeros] if g == 0 else [zeros, qg], axis=0)
            st = _dot(kwin, qpad) + bias
            vext = jnp.concatenate([vwin[g * SWA_HD:(g + 1) * SWA_HD], ones], axis=0)
            logits.append((blk, g, st, vext))
    weights = []
    for blk, g, st, vext in logits:
        sk = sink_ref[g:g + 1, :]
        m = jnp.maximum(jnp.max(st, axis=0, keepdims=True), sk)
        weights.append((blk, g, jnp.exp2((st - m).astype(BF16)), jnp.exp2(sk - m), vext))
    for blk, g, p, p_sink, vext in weights:
        acc = _dot(vext, p)
        ot = acc[0:SWA_HD] / (acc[SWA_HD:SWA_HD + 1] + p_sink)
        for j in range(SWA_GROUP):
            hd = g * SWA_GROUP + j
            ot_ref[hd * SWA_HD:(hd + 1) * SWA_HD, blk * BLOCK:(blk + 1) * BLOCK] = (
                ot[:, j * BLOCK:(j + 1) * BLOCK].astype(BF16))


def _swa_attn(qt, k, vt, sink_rows, t):
    b, hw, s = qt.shape
    kvw = SWA_KV_HEADS * SWA_HD
    bias = _swa_bias()
    return pl.pallas_call(
        _swa_kernel, grid=(b, s // t),
        in_specs=[pl.BlockSpec((None, hw, t), lambda i, j: (i, 0, j)),
                  pl.BlockSpec((None, s, kvw), lambda i, j: (i, 0, 0)),
                  pl.BlockSpec((None, kvw, s), lambda i, j: (i, 0, 0)),
                  _const_spec(sink_rows.shape), _const_spec(bias.shape)],
        out_specs=pl.BlockSpec((None, hw, t), lambda i, j: (i, 0, j)),
        out_shape=jax.ShapeDtypeStruct((b, hw, s), BF16),
        compiler_params=_params(("parallel", "arbitrary")), name="swa_attn",
    )(qt, k, vt, sink_rows, bias)


def _sigmoid(v):
    return 1.0 / (1.0 + jnp.exp(-v))


def _merge_kernel(x_ref, ot_mla_ref, ot_swa_ref, q_mem_ref, kt_mem_ref, v_mem_ref, gmix_ref, wg_ref,
                  wo_mla_ref, wo_swa_ref, wo_mem_ref, wout_ref, y_ref):
    x = x_ref[...]
    hb = (x * lax.rsqrt(jnp.mean(x * x, axis=-1, keepdims=True) + EPS) * gmix_ref[...]).astype(BF16)

    heads = [slice(hd * MEM_HD, (hd + 1) * MEM_HD) for hd in range(MEM_HEADS)]
    logits = [_dot(q_mem_ref[:, hs], kt_mem_ref[hs, :]) for hs in heads]
    probs = [jnp.exp2(sc - jnp.max(sc, axis=-1, keepdims=True)) for sc in logits]
    o_heads = [_dot(p.astype(BF16), v_mem_ref[:, hs]) / jnp.sum(p, axis=-1, keepdims=True)
               for p, hs in zip(probs, heads)]
    o_mem = jnp.concatenate(o_heads, axis=1).astype(BF16)

    merged = _sigmoid(_dot(hb, wg_ref[:, 0:D_MODEL])) * _dot_tn(ot_mla_ref[...], wo_mla_ref[...])
    merged += _sigmoid(_dot(hb, wg_ref[:, D_MODEL:2 * D_MODEL])) * _dot_tn(ot_swa_ref[...], wo_swa_ref[...])
    merged += _sigmoid(_dot(hb, wg_ref[:, 2 * D_MODEL:3 * D_MODEL])) * _dot(o_mem, wo_mem_ref[...])
    y_ref[...] = x + _dot(merged.astype(BF16), wout_ref[...])


def _merge(x, ot_mla, ot_swa, q_mem, kt_mem, v_mem, w, tm):
    b, s, d = x.shape
    n_mem = v_mem.shape[1]
    hw = MEM_HEADS * MEM_HD
    tok = lambda width: pl.BlockSpec((None, tm, width), lambda i, j: (i, j, 0))
    tok_t = lambda rows: pl.BlockSpec((None, rows, tm), lambda i, j: (i, 0, j))
    consts = [w['g_mix'], w['wg'], w['wo_mla'], w['wo_swa'], w['wo_mem'], w['wout']]
    return pl.pallas_call(
        _merge_kernel, grid=(b, s // tm),
        in_specs=[tok(d), tok_t(MLA_HEADS * MLA_V), tok_t(SWA_HEADS * SWA_HD), tok(hw),
                  pl.BlockSpec((None, hw, n_mem), lambda i, j: (i, 0, 0)),
                  pl.BlockSpec((None, n_mem, hw), lambda i, j: (i, 0, 0))] + [_const_spec(c.shape) for c in consts],
        out_specs=tok(d), out_shape=jax.ShapeDtypeStruct((b, s, d), F32),
        compiler_params=_params(("parallel", "parallel")), name="merge",
    )(x, ot_mla, ot_swa, q_mem, kt_mem, v_mem, *consts)


HALO = 8


def _ffn_kernel(x_ref, prev_ref, next_ref, gffn_ref, wup_ref, cw_ref, cb_ref, wdown_ref, y_ref):
    j = pl.program_id(1)
    nj = pl.num_programs(1)
    x = x_ref[...]
    tm = x.shape[0]
    g = gffn_ref[...]

    def norm(v):
        return v * lax.rsqrt(jnp.mean(v * v, axis=-1, keepdims=True) + EPS) * g

    hp = jnp.where(j > 0, norm(prev_ref[...]), 0.0)
    hn = jnp.where(j < nj - 1, norm(next_ref[...]), 0.0)
    hext = jnp.concatenate([hp, norm(x), hn], axis=0).astype(BF16)
    u = _dot(hext, wup_ref[...])
    cw = cw_ref[...]
    conv = (u[HALO - 1:HALO - 1 + tm] * cw[0:1] + u[HALO:HALO + tm] * cw[1:2]
            + u[HALO + 1:HALO + 1 + tm] * cw[2:3] + cb_ref[...])
    a = conv[:, :D_FF]
    val = conv[:, D_FF:]
    act = (a * _sigmoid(a) * val).astype(BF16)
    y_ref[...] = x + _dot(act, wdown_ref[...])


def _ffn(x, w, tm):
    b, s, d = x.shape
    nh = tm // HALO
    last = s // HALO - 1
    consts = [w['g_ffn'], w['wup'], w['conv_w'], w['conv_b'], w['wdown']]
    return pl.pallas_call(
        _ffn_kernel, grid=(b, s // tm),
        in_specs=[pl.BlockSpec((None, tm, d), lambda i, j: (i, j, 0)),
                  pl.BlockSpec((None, HALO, d), lambda i, j: (i, jnp.maximum(j * nh - 1, 0), 0)),
                  pl.BlockSpec((None, HALO, d), lambda i, j: (i, jnp.minimum((j + 1) * nh, last), 0))]
                 + [_const_spec(c.shape) for c in consts],
        out_specs=pl.BlockSpec((None, tm, d), lambda i, j: (i, j, 0)),
        out_shape=jax.ShapeDtypeStruct((b, s, d), F32),
        compiler_params=_params(("parallel", "parallel")), name="ffn",
    )(x, x, x, *consts)


def _prep_weights(g_mix, g_mem, w_in, q_a_norm, w_q_b, kv_a_norm, w_kv_b, g_q_mla, g_k_mla, g_q_swa, g_k_swa,
                  swa_sink, w_mem_kv, g_q_mem, g_k_mem, w_o_mla, w_o_swa, w_o_mem, w_out, g_ffn, w_up, conv_w,
                  conv_b, w_down):
    offs = [0]
    for sp in SPLITS:
        offs.append(offs[-1] + sp)
    w_cq, w_ckv, w_kr, w_qs, w_ks, w_vs, w_qm, w_gate = (w_in[:, offs[i]:offs[i + 1]] for i in range(8))
    row = lambda v: v.reshape(1, -1).astype(F32)
    col = lambda v: v.reshape(-1, 1).astype(F32)
    w = {}
    w['g_mix'] = row(g_mix)
    w['g_mem'] = row(g_mem)
    w['g_ffn'] = row(g_ffn)
    w['q_a_norm'] = row(q_a_norm)
    w['kv_a_norm'] = row(kv_a_norm)
    w_kr_placed = jnp.pad(w_kr, ((0, 0), (MLA_NOPE, MLA_HP - MLA_QK)))
    w['wa'] = jnp.concatenate([w_cq, w_ckv, w_ks, w_qm, w_kr_placed], axis=1).astype(BF16)
    w['wbt'] = jnp.concatenate([w_qs, w_vs], axis=1).T.astype(BF16)
    w['wg'] = w_gate.astype(BF16)
    wq = jnp.pad(w_q_b.reshape(Q_LORA, MLA_HEADS, MLA_QK), ((0, 0), (0, 0), (0, MLA_HP - MLA_QK)))
    w['wqt'] = wq.reshape(Q_LORA, MLA_HEADS * MLA_HP).T.astype(BF16)
    wkv = w_kv_b.reshape(KV_LORA, MLA_HEADS, MLA_NOPE + MLA_V)
    wk_nope = jnp.pad(wkv[:, :, :MLA_NOPE], ((0, 0), (0, 0), (0, MLA_HP - MLA_NOPE))).reshape(KV_LORA, -1)
    w['wk'] = wk_nope.astype(BF16)
    w['wvt'] = wkv[:, :, MLA_NOPE:].reshape(KV_LORA, MLA_HEADS * MLA_V).T.astype(BF16)
    sc_mla = MLA_QK ** -0.5 * LOG2E
    w['gq_mla'] = col(jnp.pad(g_q_mla * sc_mla, (0, MLA_HP - MLA_QK)))
    w['gk_mla'] = row(jnp.pad(g_k_mla, (0, MLA_HP - MLA_QK)))
    w['mla_logit_bound'] = 1.02 * MLA_QK * jnp.max(jnp.abs(w['gq_mla'])) * jnp.max(jnp.abs(w['gk_mla']))
    w['gq_swa'] = col(g_q_swa * (SWA_HD ** -0.5 * LOG2E))
    w['gk_swa'] = row(jnp.tile(g_k_swa, SWA_KV_HEADS))
    w['gq_mem'] = row(g_q_mem * (MEM_HD ** -0.5 * LOG2E))
    w['gk_mem'] = col(g_k_mem)
    w['sink_rows'] = jnp.repeat(swa_sink.astype(F32) * LOG2E, BLOCK).reshape(SWA_KV_HEADS, SWA_GROUP * BLOCK)
    hw = MEM_HEADS * MEM_HD
    w['wmkt'] = w_mem_kv[:, :hw].T.astype(BF16)
    w['wmv'] = w_mem_kv[:, hw:].astype(BF16)
    w['wo_mla'] = w_o_mla.astype(BF16)
    w['wo_swa'] = w_o_swa.astype(BF16)
    w['wo_mem'] = w_o_mem.astype(BF16)
    w['wout'] = w_out.astype(BF16)
    w['wup'] = w_up.astype(BF16)
    w['wdown'] = w_down.astype(BF16)
    w['conv_w'] = conv_w.astype(F32)
    w['conv_b'] = row(conv_b)
    return w


def _rope_tables(s):
    def cs(dim):
        inv = 1.0 / (ROPE_THETA ** (jnp.arange(0, dim, 2, dtype=F32) / dim))
        ang = jnp.arange(s, dtype=F32)[:, None] * inv[None, :]
        return jnp.cos(ang), jnp.sin(ang)

    c16, s16 = cs(MLA_ROPE)
    c32, s32 = cs(SWA_HD)
    z = lambda n: jnp.zeros((s, n), F32)
    t = {'ct16': c16.T, 'st16': s16.T, 'ct32': c32.T, 'st32': s32.T}
    t['ck'] = jnp.concatenate([jnp.ones((s, MLA_NOPE), F32), c16, c16, z(MLA_HP - MLA_QK)], axis=1)
    t['sak'] = jnp.concatenate([z(MLA_NOPE), -s16, z(16), z(MLA_HP - MLA_QK)], axis=1)
    t['sbk'] = jnp.concatenate([z(MLA_NOPE), z(16), s16, z(MLA_HP - MLA_QK)], axis=1)
    t['cs'] = jnp.concatenate([c32, c32] * SWA_KV_HEADS, axis=1)
    t['sas'] = jnp.concatenate([-s32, z(32)] * SWA_KV_HEADS, axis=1)
    t['sbs'] = jnp.concatenate([z(32), s32] * SWA_KV_HEADS, axis=1)
    return t


def _tiles(s):
    pick = lambda want: want if s % want == 0 else BLOCK
    mla_k = 1024 if s % 2048 == 0 else BLOCK
    return dict(proj=pick(512), mla_q=pick(256), mla_q_bounded=pick(512), mla_k=mla_k, swa=pick(1024),
                merge=pick(1024), ffn=pick(512))


def _layer(x, mem, w, tabs):
    b, s, d = x.shape
    assert d == D_MODEL and s % BLOCK == 0 and s >= 3 * BLOCK
    ts = _tiles(s)
    qt_mla, k_mla, vt_mla, qt_swa, k_swa, vt_swa, q_mem = _proj(x, w, tabs, ts['proj'])
    kt_mem, v_mem = _mem_kv(mem, w)
    ot_mla = lax.cond(
        w['mla_logit_bound'] <= MLA_LOGIT_BOUND,
        lambda: _mla_attn(qt_mla, k_mla, vt_mla, ts['mla_q_bounded'], ts['mla_k'], True),
        lambda: _mla_attn(qt_mla, k_mla, vt_mla, ts['mla_q'], ts['mla_k'], False))
    ot_swa = _swa_attn(qt_swa, k_swa, vt_swa, w['sink_rows'], ts['swa'])
    x1 = _merge(x, ot_mla, ot_swa, q_mem, kt_mem, v_mem, w, ts['merge'])
    return _ffn(x1, w, ts['ffn'])


def kernel(x_prompt, x_sample, mem_prompt, mem_sample, g_mix, g_mem, w_in, q_a_norm, w_q_b, kv_a_norm, w_kv_b,
           g_q_mla, g_k_mla, g_q_swa, g_k_swa, swa_sink, w_mem_kv, g_q_mem, g_k_mem, w_o_mla, w_o_swa, w_o_mem,
           w_out, g_ffn, w_up, conv_w, conv_b, w_down):
    weights = (g_mix, g_mem, w_in, q_a_norm, w_q_b, kv_a_norm, w_kv_b, g_q_mla, g_k_mla, g_q_swa, g_k_swa,
               swa_sink, w_mem_kv, g_q_mem, g_k_mem, w_o_mla, w_o_swa, w_o_mem, w_out, g_ffn, w_up, conv_w,
               conv_b, w_down)
    depth = g_mix.shape[0]
    y_prompt, y_sample = x_prompt, x_sample
    tabs = _rope_tables(max(x_prompt.shape[1], x_sample.shape[1]))
    for layer in range(depth):
        w = _prep_weights(*(p[layer] for p in weights))
        y_prompt = _layer(y_prompt, mem_prompt, w, tabs)
        y_sample = _layer(y_sample, mem_sample, w, tabs)
    return (y_prompt, y_sample)
```

```python
import functools
import math

import jax
import jax.numpy as jnp
from jax import lax
from jax.experimental import pallas as pl
from jax.experimental.pallas import tpu as pltpu

D_MODEL = 1024
N_MEM = 256
EPS = 1e-6
ROPE_THETA = 10000.0
MLA_HEADS = 8
MLA_NOPE = 64
MLA_ROPE = 32
MLA_V = 64
MLA_QK = MLA_NOPE + MLA_ROPE
Q_LORA = 384
KV_LORA = 256
SWA_HEADS = 8
SWA_KV_HEADS = 2
SWA_GROUP = SWA_HEADS // SWA_KV_HEADS
SWA_HD = 64
WINDOW = 128
BLOCK = 128
MEM_HEADS = 4
MEM_HD = 128
N_BRANCH = 3
D_FF = 2816
SPLITS = (Q_LORA, KV_LORA, MLA_ROPE, SWA_HEADS * SWA_HD, SWA_KV_HEADS * SWA_HD,
          SWA_KV_HEADS * SWA_HD, MEM_HEADS * MEM_HD, N_BRANCH * D_MODEL)

LANE = 128
MLA_HP = 128
MLA_VROWS = 80
LOG2E = math.log2(math.e)
NEG_BIG = -1e30
MLA_LOGIT_BOUND = 80.0
VMEM_LIMIT = 56 * 1024 * 1024

BF16 = jnp.bfloat16
F32 = jnp.float32


def _dot(a, b):
    return jnp.dot(a, b, preferred_element_type=F32)


def _dot_nt(a, b):
    return lax.dot_general(a, b, (((1,), (1,)), ((), ())), preferred_element_type=F32)


def _dot_tn(a, b):
    return lax.dot_general(a, b, (((0,), (0,)), ((), ())), preferred_element_type=F32)


def _const_spec(shape):
    nd = len(shape)
    return pl.BlockSpec(shape, lambda *_: (0,) * nd, pipeline_mode=pl.Buffered(1))


def _params(semantics):
    return pltpu.CompilerParams(dimension_semantics=semantics, vmem_limit_bytes=VMEM_LIMIT)


def _proj_kernel(x_ref, gmix_ref, wa_ref, wbt_ref, qan_ref, wqt_ref, kvan_ref, wk_ref, wvt_ref,
                 gq_mla_ref, gk_mla_ref, gq_swa_ref, gk_swa_ref, gq_mem_ref,
                 ct16_ref, st16_ref, ck_ref, sak_ref, sbk_ref,
                 ct32_ref, st32_ref, cs_ref, sas_ref, sbs_ref,
                 qt_mla_ref, k_mla_ref, vt_mla_ref, qt_swa_ref, k_swa_ref, vt_swa_ref, q_mem_ref):
    x = x_ref[...]
    tm = x.shape[0]
    h = x * lax.rsqrt(jnp.mean(x * x, axis=-1, keepdims=True) + EPS) * gmix_ref[...]
    hb = h.astype(BF16)
    z = _dot(hb, wa_ref[...])
    c_q = z[:, 0:384]
    c_kv = z[:, 384:640]
    k_s = z[:, 640:768]
    q_m = z[:, 768:1280]
    k_rope = z[:, 1280:1408]

    bt = _dot_nt(wbt_ref[...], hb)
    c_qn = (c_q * lax.rsqrt(jnp.mean(c_q * c_q, axis=-1, keepdims=True) + EPS) * qan_ref[...]).astype(BF16)
    c_kvn = (c_kv * lax.rsqrt(jnp.mean(c_kv * c_kv, axis=-1, keepdims=True) + EPS) * kvan_ref[...]).astype(BF16)
    qt = _dot_nt(wqt_ref[...], c_qn)
    kpre = _dot(c_kvn, wk_ref[...])
    vt = _dot_nt(wvt_ref[...], c_kvn)

    ct = ct16_ref[...]
    st = st16_ref[...]
    gq = gq_mla_ref[...]
    for hd in range(MLA_HEADS):
        blk = qt[hd * MLA_HP:(hd + 1) * MLA_HP]
        ss = jnp.sum(blk * blk, axis=0, keepdims=True)
        y = blk * lax.rsqrt(ss * (1.0 / MLA_QK) + EPS) * gq
        x1 = y[64:80]
        x2 = y[80:96]
        out = jnp.concatenate([y[0:64], x1 * ct - x2 * st, x2 * ct + x1 * st, y[96:128]], axis=0)
        qt_mla_ref[hd * MLA_HP:(hd + 1) * MLA_HP, :] = out.astype(BF16)

    gk = gk_mla_ref[...]
    ss_rope = jnp.sum(k_rope * k_rope, axis=-1, keepdims=True)
    yr = k_rope * gk
    k_roped = yr * ck_ref[...] + pltpu.roll(yr, 112, 1) * sak_ref[...] + pltpu.roll(yr, 16, 1) * sbk_ref[...]
    for hd in range(MLA_HEADS):
        blk = kpre[:, hd * MLA_HP:(hd + 1) * MLA_HP]
        ss = jnp.sum(blk * blk, axis=-1, keepdims=True) + ss_rope
        out = (blk * gk + k_roped) * lax.rsqrt(ss * (1.0 / MLA_QK) + EPS)
        k_mla_ref[hd] = out.astype(BF16)
    ones = jnp.ones((MLA_VROWS - MLA_V, tm), BF16)
    for hd in range(MLA_HEADS):
        vt_mla_ref[hd * MLA_VROWS:hd * MLA_VROWS + MLA_V, :] = vt[hd * MLA_V:(hd + 1) * MLA_V].astype(BF16)
        vt_mla_ref[hd * MLA_VROWS + MLA_V:(hd + 1) * MLA_VROWS, :] = ones

    c32 = ct32_ref[...]
    s32 = st32_ref[...]
    gqs = gq_swa_ref[...]
    for hd in range(SWA_HEADS):
        blk = bt[hd * SWA_HD:(hd + 1) * SWA_HD]
        ss = jnp.sum(blk * blk, axis=0, keepdims=True)
        y = blk * lax.rsqrt(ss * (1.0 / SWA_HD) + EPS) * gqs
        x1 = y[0:32]
        x2 = y[32:64]
        out = jnp.concatenate([x1 * c32 - x2 * s32, x2 * c32 + x1 * s32], axis=0)
        qt_swa_ref[hd * SWA_HD:(hd + 1) * SWA_HD, :] = out.astype(BF16)
    vt_swa_ref[...] = bt[SWA_HEADS * SWA_HD:].astype(BF16)
    lane = lax.broadcasted_iota(jnp.int32, (tm, LANE), 1)
    lo = lane < SWA_HD
    sq = k_s * k_s
    ss_lo = jnp.sum(jnp.where(lo, sq, 0.0), axis=-1, keepdims=True)
    ss_hi = jnp.sum(jnp.where(lo, 0.0, sq), axis=-1, keepdims=True)
    rstd = jnp.where(lo, lax.rsqrt(ss_lo * (1.0 / SWA_HD) + EPS), lax.rsqrt(ss_hi * (1.0 / SWA_HD) + EPS))
    y = k_s * rstd * gk_swa_ref[...]
    out = y * cs_ref[...] + pltpu.roll(y, 96, 1) * sas_ref[...] + pltpu.roll(y, 32, 1) * sbs_ref[...]
    k_swa_ref[...] = out.astype(BF16)

    gqm = gq_mem_ref[...]
    for hd in range(MEM_HEADS):
        blk = q_m[:, hd * MEM_HD:(hd + 1) * MEM_HD]
        ss = jnp.sum(blk * blk, axis=-1, keepdims=True)
        y = blk * lax.rsqrt(ss * (1.0 / MEM_HD) + EPS) * gqm
        q_mem_ref[:, hd * MEM_HD:(hd + 1) * MEM_HD] = y.astype(BF16)


def _proj(x, w, tabs, tm):
    b, s, d = x.shape
    grid = (s // tm, b)
    tok = lambda width: pl.BlockSpec((None, tm, width), lambda j, i: (i, j, 0))
    tok_t = lambda rows: pl.BlockSpec((None, rows, tm), lambda j, i: (i, 0, j))
    tab = lambda width: pl.BlockSpec((tm, width), lambda j, i: (j, 0))
    tab_t = lambda rows: pl.BlockSpec((rows, tm), lambda j, i: (0, j))
    consts = [w['g_mix'], w['wa'], w['wbt'], w['q_a_norm'], w['wqt'], w['kv_a_norm'], w['wk'], w['wvt'],
              w['gq_mla'], w['gk_mla'], w['gq_swa'], w['gk_swa'], w['gq_mem']]
    in_specs = [tok(d)] + [_const_spec(c.shape) for c in consts] + [
        tab_t(16), tab_t(16), tab(LANE), tab(LANE), tab(LANE),
        tab_t(32), tab_t(32), tab(LANE), tab(LANE), tab(LANE)]
    out_shape = (
        jax.ShapeDtypeStruct((b, MLA_HEADS * MLA_HP, s), BF16),
        jax.ShapeDtypeStruct((b, MLA_HEADS, s, MLA_HP), BF16),
        jax.ShapeDtypeStruct((b, MLA_HEADS * MLA_VROWS, s), BF16),
        jax.ShapeDtypeStruct((b, SWA_HEADS * SWA_HD, s), BF16),
        jax.ShapeDtypeStruct((b, s, SWA_KV_HEADS * SWA_HD), BF16),
        jax.ShapeDtypeStruct((b, SWA_KV_HEADS * SWA_HD, s), BF16),
        jax.ShapeDtypeStruct((b, s, MEM_HEADS * MEM_HD), BF16),
    )
    k_heads = pl.BlockSpec((None, MLA_HEADS, tm, MLA_HP), lambda j, i: (i, 0, j, 0))
    out_specs = (tok_t(MLA_HEADS * MLA_HP), k_heads, tok_t(MLA_HEADS * MLA_VROWS),
                 tok_t(SWA_HEADS * SWA_HD), tok(SWA_KV_HEADS * SWA_HD), tok_t(SWA_KV_HEADS * SWA_HD),
                 tok(MEM_HEADS * MEM_HD))
    return pl.pallas_call(
        _proj_kernel, grid=grid, in_specs=in_specs, out_specs=out_specs, out_shape=out_shape,
        compiler_params=_params(("parallel", "parallel")), name="proj",
    )(x, *consts, tabs['ct16'], tabs['st16'], tabs['ck'], tabs['sak'], tabs['sbk'],
      tabs['ct32'], tabs['st32'], tabs['cs'], tabs['sas'], tabs['sbs'])


def _mem_kv_kernel(mem_ref, gmem_ref, wkt_ref, wv_ref, gk_ref, kt_ref, v_ref):
    m = mem_ref[...]
    mn = (m * lax.rsqrt(jnp.mean(m * m, axis=-1, keepdims=True) + EPS) * gmem_ref[...]).astype(BF16)
    kt = _dot_nt(wkt_ref[...], mn)
    gk = gk_ref[...]
    for hd in range(MEM_HEADS):
        blk = kt[hd * MEM_HD:(hd + 1) * MEM_HD]
        ss = jnp.sum(blk * blk, axis=0, keepdims=True)
        kt_ref[hd * MEM_HD:(hd + 1) * MEM_HD, :] = (blk * lax.rsqrt(ss * (1.0 / MEM_HD) + EPS) * gk).astype(BF16)
    v_ref[...] = _dot(mn, wv_ref[...]).astype(BF16)


def _mem_kv(mem, w):
    b, n, d = mem.shape
    hw = MEM_HEADS * MEM_HD
    consts = [w['g_mem'], w['wmkt'], w['wmv'], w['gk_mem']]
    return pl.pallas_call(
        _mem_kv_kernel, grid=(b,),
        in_specs=[pl.BlockSpec((None, n, d), lambda i: (i, 0, 0))] + [_const_spec(c.shape) for c in consts],
        out_specs=(pl.BlockSpec((None, hw, n), lambda i: (i, 0, 0)), pl.BlockSpec((None, n, hw), lambda i: (i, 0, 0))),
        out_shape=(jax.ShapeDtypeStruct((b, hw, n), BF16), jax.ShapeDtypeStruct((b, n, hw), BF16)),
        compiler_params=_params(("parallel",)), name="mem_kv",
    )(mem, *consts)


def _aligned(start, align):
    return start if isinstance(start, int) else pl.multiple_of(start, align)


def _mla_kernel(qt_ref, k_ref, vt_ref, ot_ref, *scratch, tile_fn, tq):
    def body(i, carry):
        cols = pl.ds(pl.multiple_of(i * tq, tq), tq)
        ot_ref[:, cols] = tile_fn(qt_ref[:, cols], k_ref, vt_ref, *scratch)
        return carry

    lax.fori_loop(0, qt_ref.shape[1] // tq, body, 0)


def _mla_online_tile(qt, k_ref, vt_ref, s0_ref, s1_ref, p0_ref, p1_ref, *, tk, sub):
    s = k_ref.shape[0]
    tq = qt.shape[1]
    n = s // tk
    s_refs = (s0_ref, s1_ref)
    p_refs = (p0_ref, p1_ref)

    def step(c, par, carry, logits=True, accum=True, exps=True):
        mc, m, alpha, acc = carry
        if exps:
            m_new = jnp.maximum(m, mc)
            alpha_new = jnp.exp2(m - m_new)
        if accum:
            acc = alpha * acc
        mc_next = None
        for j in range(tk // sub):
            blk = slice(j * sub, (j + 1) * sub)
            if logits:
                off = _aligned((c + 1) * tk + j * sub, sub)
                st = _dot(k_ref[pl.ds(off, sub), :], qt)
                s_refs[1 - par][blk, :] = st
                mj = jnp.max(st, axis=0, keepdims=True)
                mc_next = mj if mc_next is None else jnp.maximum(mc_next, mj)
            if accum:
                off = _aligned((c - 1) * tk + j * sub, sub)
                acc = acc + _dot(vt_ref[:, pl.ds(off, sub)], p_refs[1 - par][blk, :])
            if exps:
                p_refs[par][blk, :] = jnp.exp2((s_refs[par][blk, :] - m_new).astype(BF16))
        return (mc_next if logits else mc, m_new if exps else m, alpha_new if exps else alpha, acc)

    def pair(i, carry):
        return step(2 * i + 2, 0, step(2 * i + 1, 1, carry))

    init = jnp.full((1, tq), NEG_BIG, F32)
    carry = (init, init, jnp.zeros((1, tq), F32), jnp.zeros((MLA_VROWS, tq), F32))
    carry = step(-1, 1, carry, accum=False, exps=False)
    carry = step(0, 0, carry, accum=False)
    carry = lax.fori_loop(0, n // 2 - 1, pair, carry, unroll=True)
    carry = step(n - 1, 1, carry, logits=False)
    _, _, _, acc = step(n, 0, carry, logits=False, exps=False)
    return (acc[0:MLA_V] / acc[MLA_V:MLA_V + 1]).astype(BF16)


def _mla_bounded_tile(qt, k_ref, vt_ref, *, sub):
    s = k_ref.shape[0]
    acc = jnp.zeros((MLA_VROWS, qt.shape[1]), F32)
    for j in range(s // sub):
        blk = slice(j * sub, (j + 1) * sub)
        pt = jnp.exp2(_dot(k_ref[blk, :], qt)).astype(BF16)
        acc = acc + _dot(vt_ref[:, blk], pt)
    return (acc[0:MLA_V] / acc[MLA_V:MLA_V + 1]).astype(BF16)


def _mla_attn(qt, k, vt, tq, tk, bounded):
    b, _, s = qt.shape
    tq_step = min(s, 4 * tq)
    grid = (b, MLA_HEADS, s // tq_step)
    assert (s // tk) % 2 == 0 and s % tq_step == 0
    if bounded:
        tile_fn = functools.partial(_mla_bounded_tile, sub=min(s, 2048))
        scratch = []
    else:
        tile_fn = functools.partial(_mla_online_tile, tk=tk, sub=min(tk, 256))
        scratch = [pltpu.VMEM((tk, tq), F32), pltpu.VMEM((tk, tq), F32),
                   pltpu.VMEM((tk, tq), BF16), pltpu.VMEM((tk, tq), BF16)]
    return pl.pallas_call(
        functools.partial(_mla_kernel, tile_fn=tile_fn, tq=tq), grid=grid,
        in_specs=[pl.BlockSpec((None, MLA_HP, tq_step), lambda i, h, j: (i, h, j)),
                  pl.BlockSpec((None, None, s, MLA_HP), lambda i, h, j: (i, h, 0, 0)),
                  pl.BlockSpec((None, MLA_VROWS, s), lambda i, h, j: (i, h, 0))],
        out_specs=pl.BlockSpec((None, MLA_V, tq_step), lambda i, h, j: (i, h, j)),
        out_shape=jax.ShapeDtypeStruct((b, MLA_HEADS * MLA_V, s), BF16),
        scratch_shapes=scratch,
        compiler_params=_params(("parallel", "parallel", "arbitrary")),
        name="mla_attn_bounded" if bounded else "mla_attn",
    )(qt, k, vt)


SWA_SPAN = 3 * BLOCK


def _swa_bias():
    r = jnp.arange(SWA_SPAN)[:, None]
    c = jnp.arange(BLOCK)[None, :]
    return jnp.stack([jnp.where(jnp.abs(lead * BLOCK + c - r) <= WINDOW, 0.0, NEG_BIG) for lead in range(3)]).astype(F32)


def _swa_kernel(qt_ref, k_ref, vt_ref, sink_ref, bias_ref, ot_ref):
    s = k_ref.shape[0]
    t = qt_ref.shape[1]
    nb = s // BLOCK
    gw = SWA_GROUP * BLOCK
    zeros = jnp.zeros((SWA_HD, gw), BF16)
    ones = jnp.ones((16, SWA_SPAN), BF16)
    logits = []
    for blk in range(t // BLOCK):
        n = pl.program_id(1) * (t // BLOCK) + blk
        start = pl.multiple_of(jnp.clip((n - 1) * BLOCK, 0, s - SWA_SPAN), BLOCK)
        kwin = k_ref[pl.ds(start, SWA_SPAN), :]
        vwin = vt_ref[:, pl.ds(start, SWA_SPAN)]
        lead = jnp.where(n == 0, 0, jnp.where(n == nb - 1, 2, 1))
        bias = bias_ref[lead]
        bias = jnp.concatenate([bias] * SWA_GROUP, axis=1)
        for g in range(SWA_KV_HEADS):
            qg = jnp.concatenate(
                [qt_ref[(g * SWA_GROUP + j) * SWA_HD:(g * SWA_GROUP + j + 1) * SWA_HD, blk * BLOCK:(blk + 1) * BLOCK]
                 for j in range(SWA_GROUP)], axis=1)
            qpad = jnp.concatenate([qg, zeros] if g == 0 else [zeros, qg], axis=0)
            st = _dot(kwin, qpad) + bias
            vext = jnp.concatenate([vwin[g * SWA_HD:(g + 1) * SWA_HD], ones], axis=0)
            logits.append((blk, g, st, vext))
    weights = []
    for blk, g, st, vext in logits:
        sk = sink_ref[g:g + 1, :]
        m = jnp.maximum(jnp.max(st, axis=0, keepdims=True), sk)
        weights.append((blk, g, jnp.exp2((st - m).astype(BF16)), jnp.exp2(sk - m), vext))
    for blk, g, p, p_sink, vext in weights:
        acc = _dot(vext, p)
        ot = acc[0:SWA_HD] / (acc[SWA_HD:SWA_HD + 1] + p_sink)
        for j in range(SWA_GROUP):
            hd = g * SWA_GROUP + j
            ot_ref[hd * SWA_HD:(hd + 1) * SWA_HD, blk * BLOCK:(blk + 1) * BLOCK] = (
                ot[:, j * BLOCK:(j + 1) * BLOCK].astype(BF16))


def _swa_attn(qt, k, vt, sink_rows, t):
    b, hw, s = qt.shape
    kvw = SWA_KV_HEADS * SWA_HD
    bias = _swa_bias()
    return pl.pallas_call(
        _swa_kernel, grid=(b, s // t),
        in_specs=[pl.BlockSpec((None, hw, t), lambda i, j: (i, 0, j)),
                  pl.BlockSpec((None, s, kvw), lambda i, j: (i, 0, 0)),
                  pl.BlockSpec((None, kvw, s), lambda i, j: (i, 0, 0)),
                  _const_spec(sink_rows.shape), _const_spec(bias.shape)],
        out_specs=pl.BlockSpec((None, hw, t), lambda i, j: (i, 0, j)),
        out_shape=jax.ShapeDtypeStruct((b, hw, s), BF16),
        compiler_params=_params(("parallel", "arbitrary")), name="swa_attn",
    )(qt, k, vt, sink_rows, bias)


def _sigmoid(v):
    return 1.0 / (1.0 + jnp.exp(-v))


def _merge_kernel(x_ref, ot_mla_ref, ot_swa_ref, q_mem_ref, kt_mem_ref, v_mem_ref, gmix_ref, wg_ref,
                  wo_mla_ref, wo_swa_ref, wo_mem_ref, wout_ref, y_ref):
    x = x_ref[...]
    hb = (x * lax.rsqrt(jnp.mean(x * x, axis=-1, keepdims=True) + EPS) * gmix_ref[...]).astype(BF16)

    heads = [slice(hd * MEM_HD, (hd + 1) * MEM_HD) for hd in range(MEM_HEADS)]
    logits = [_dot(q_mem_ref[:, hs], kt_mem_ref[hs, :]) for hs in heads]
    probs = [jnp.exp2(sc - jnp.max(sc, axis=-1, keepdims=True)) for sc in logits]
    o_heads = [_dot(p.astype(BF16), v_mem_ref[:, hs]) / jnp.sum(p, axis=-1, keepdims=True)
               for p, hs in zip(probs, heads)]
    o_mem = jnp.concatenate(o_heads, axis=1).astype(BF16)

    merged = _sigmoid(_dot(hb, wg_ref[:, 0:D_MODEL])) * _dot_tn(ot_mla_ref[...], wo_mla_ref[...])
    merged += _sigmoid(_dot(hb, wg_ref[:, D_MODEL:2 * D_MODEL])) * _dot_tn(ot_swa_ref[...], wo_swa_ref[...])
    merged += _sigmoid(_dot(hb, wg_ref[:, 2 * D_MODEL:3 * D_MODEL])) * _dot(o_mem, wo_mem_ref[...])
    y_ref[...] = x + _dot(merged.astype(BF16), wout_ref[...])


def _merge(x, ot_mla, ot_swa, q_mem, kt_mem, v_mem, w, tm):
    b, s, d = x.shape
    n_mem = v_mem.shape[1]
    hw = MEM_HEADS * MEM_HD
    tok = lambda width: pl.BlockSpec((None, tm, width), lambda i, j: (i, j, 0))
    tok_t = lambda rows: pl.BlockSpec((None, rows, tm), lambda i, j: (i, 0, j))
    consts = [w['g_mix'], w['wg'], w['wo_mla'], w['wo_swa'], w['wo_mem'], w['wout']]
    return pl.pallas_call(
        _merge_kernel, grid=(b, s // tm),
        in_specs=[tok(d), tok_t(MLA_HEADS * MLA_V), tok_t(SWA_HEADS * SWA_HD), tok(hw),
                  pl.BlockSpec((None, hw, n_mem), lambda i, j: (i, 0, 0)),
                  pl.BlockSpec((None, n_mem, hw), lambda i, j: (i, 0, 0))] + [_const_spec(c.shape) for c in consts],
        out_specs=tok(d), out_shape=jax.ShapeDtypeStruct((b, s, d), F32),
        compiler_params=_params(("parallel", "parallel")), name="merge",
    )(x, ot_mla, ot_swa, q_mem, kt_mem, v_mem, *consts)


HALO = 8


def _ffn_kernel(x_ref, prev_ref, next_ref, gffn_ref, wup_ref, cw_ref, cb_ref, wdown_ref, y_ref):
    j = pl.program_id(1)
    nj = pl.num_programs(1)
    x = x_ref[...]
    tm = x.shape[0]
    g = gffn_ref[...]

    def norm(v):
        return v * lax.rsqrt(jnp.mean(v * v, axis=-1, keepdims=True) + EPS) * g

    hp = jnp.where(j > 0, norm(prev_ref[...]), 0.0)
    hn = jnp.where(j < nj - 1, norm(next_ref[...]), 0.0)
    hext = jnp.concatenate([hp, norm(x), hn], axis=0).astype(BF16)
    u = _dot(hext, wup_ref[...])
    cw = cw_ref[...]
    conv = (u[HALO - 1:HALO - 1 + tm] * cw[0:1] + u[HALO:HALO + tm] * cw[1:2]
            + u[HALO + 1:HALO + 1 + tm] * cw[2:3] + cb_ref[...])
    a = conv[:, :D_FF]
    val = conv[:, D_FF:]
    act = (a * _sigmoid(a) * val).astype(BF16)
    y_ref[...] = x + _dot(act, wdown_ref[...])


def _ffn(x, w, tm):
    b, s, d = x.shape
    nh = tm // HALO
    last = s // HALO - 1
    consts = [w['g_ffn'], w['wup'], w['conv_w'], w['conv_b'], w['wdown']]
    return pl.pallas_call(
        _ffn_kernel, grid=(b, s // tm),
        in_specs=[pl.BlockSpec((None, tm, d), lambda i, j: (i, j, 0)),
                  pl.BlockSpec((None, HALO, d), lambda i, j: (i, jnp.maximum(j * nh - 1, 0), 0)),
                  pl.BlockSpec((None, HALO, d), lambda i, j: (i, jnp.minimum((j + 1) * nh, last), 0))]
                 + [_const_spec(c.shape) for c in consts],
        out_specs=pl.BlockSpec((None, tm, d), lambda i, j: (i, j, 0)),
        out_shape=jax.ShapeDtypeStruct((b, s, d), F32),
        compiler_params=_params(("parallel", "parallel")), name="ffn",
    )(x, x, x, *consts)


def _prep_weights(g_mix, g_mem, w_in, q_a_norm, w_q_b, kv_a_norm, w_kv_b, g_q_mla, g_k_mla, g_q_swa, g_k_swa,
                  swa_sink, w_mem_kv, g_q_mem, g_k_mem, w_o_mla, w_o_swa, w_o_mem, w_out, g_ffn, w_up, conv_w,
                  conv_b, w_down):
    offs = [0]
    for sp in SPLITS:
        offs.append(offs[-1] + sp)
    w_cq, w_ckv, w_kr, w_qs, w_ks, w_vs, w_qm, w_gate = (w_in[:, offs[i]:offs[i + 1]] for i in range(8))
    row = lambda v: v.reshape(1, -1).astype(F32)
    col = lambda v: v.reshape(-1, 1).astype(F32)
    w = {}
    w['g_mix'] = row(g_mix)
    w['g_mem'] = row(g_mem)
    w['g_ffn'] = row(g_ffn)
    w['q_a_norm'] = row(q_a_norm)
    w['kv_a_norm'] = row(kv_a_norm)
    w_kr_placed = jnp.pad(w_kr, ((0, 0), (MLA_NOPE, MLA_HP - MLA_QK)))
    w['wa'] = jnp.concatenate([w_cq, w_ckv, w_ks, w_qm, w_kr_placed], axis=1).astype(BF16)
    w['wbt'] = jnp.concatenate([w_qs, w_vs], axis=1).T.astype(BF16)
    w['wg'] = w_gate.astype(BF16)
    wq = jnp.pad(w_q_b.reshape(Q_LORA, MLA_HEADS, MLA_QK), ((0, 0), (0, 0), (0, MLA_HP - MLA_QK)))
    w['wqt'] = wq.reshape(Q_LORA, MLA_HEADS * MLA_HP).T.astype(BF16)
    wkv = w_kv_b.reshape(KV_LORA, MLA_HEADS, MLA_NOPE + MLA_V)
    wk_nope = jnp.pad(wkv[:, :, :MLA_NOPE], ((0, 0), (0, 0), (0, MLA_HP - MLA_NOPE))).reshape(KV_LORA, -1)
    w['wk'] = wk_nope.astype(BF16)
    w['wvt'] = wkv[:, :, MLA_NOPE:].reshape(KV_LORA, MLA_HEADS * MLA_V).T.astype(BF16)
    sc_mla = MLA_QK ** -0.5 * LOG2E
    w['gq_mla'] = col(jnp.pad(g_q_mla * sc_mla, (0, MLA_HP - MLA_QK)))
    w['gk_mla'] = row(jnp.pad(g_k_mla, (0, MLA_HP - MLA_QK)))
    w['mla_logit_bound'] = 1.02 * MLA_QK * jnp.max(jnp.abs(w['gq_mla'])) * jnp.max(jnp.abs(w['gk_mla']))
    w['gq_swa'] = col(g_q_swa * (SWA_HD ** -0.5 * LOG2E))
    w['gk_swa'] = row(jnp.tile(g_k_swa, SWA_KV_HEADS))
    w['gq_mem'] = row(g_q_mem * (MEM_HD ** -0.5 * LOG2E))
    w['gk_mem'] = col(g_k_mem)
    w['sink_rows'] = jnp.repeat(swa_sink.astype(F32) * LOG2E, BLOCK).reshape(SWA_KV_HEADS, SWA_GROUP * BLOCK)
    hw = MEM_HEADS * MEM_HD
    w['wmkt'] = w_mem_kv[:, :hw].T.astype(BF16)
    w['wmv'] = w_mem_kv[:, hw:].astype(BF16)
    w['wo_mla'] = w_o_mla.astype(BF16)
    w['wo_swa'] = w_o_swa.astype(BF16)
    w['wo_mem'] = w_o_mem.astype(BF16)
    w['wout'] = w_out.astype(BF16)
    w['wup'] = w_up.astype(BF16)
    w['wdown'] = w_down.astype(BF16)
    w['conv_w'] = conv_w.astype(F32)
    w['conv_b'] = row(conv_b)
    return w


def _rope_tables(s):
    def cs(dim):
        inv = 1.0 / (ROPE_THETA ** (jnp.arange(0, dim, 2, dtype=F32) / dim))
        ang = jnp.arange(s, dtype=F32)[:, None] * inv[None, :]
        return jnp.cos(ang), jnp.sin(ang)

    c16, s16 = cs(MLA_ROPE)
    c32, s32 = cs(SWA_HD)
    z = lambda n: jnp.zeros((s, n), F32)
    t = {'ct16': c16.T, 'st16': s16.T, 'ct32': c32.T, 'st32': s32.T}
    t['ck'] = jnp.concatenate([jnp.ones((s, MLA_NOPE), F32), c16, c16, z(MLA_HP - MLA_QK)], axis=1)
    t['sak'] = jnp.concatenate([z(MLA_NOPE), -s16, z(16), z(MLA_HP - MLA_QK)], axis=1)
    t['sbk'] = jnp.concatenate([z(MLA_NOPE), z(16), s16, z(MLA_HP - MLA_QK)], axis=1)
    t['cs'] = jnp.concatenate([c32, c32] * SWA_KV_HEADS, axis=1)
    t['sas'] = jnp.concatenate([-s32, z(32)] * SWA_KV_HEADS, axis=1)
    t['sbs'] = jnp.concatenate([z(32), s32] * SWA_KV_HEADS, axis=1)
    return t


def _tiles(s):
    pick = lambda want: want if s % want == 0 else BLOCK
    mla_k = 1024 if s % 2048 == 0 else BLOCK
    return dict(proj=pick(512), mla_q=pick(256), mla_q_bounded=pick(1024), mla_k=mla_k, swa=pick(1024),
                merge=pick(1024), ffn=pick(512))


def _layer(x, mem, w, tabs):
    b, s, d = x.shape
    assert d == D_MODEL and s % BLOCK == 0 and s >= 3 * BLOCK
    ts = _tiles(s)
    qt_mla, k_mla, vt_mla, qt_swa, k_swa, vt_swa, q_mem = _proj(x, w, tabs, ts['proj'])
    kt_mem, v_mem = _mem_kv(mem, w)
    ot_mla = lax.cond(
        w['mla_logit_bound'] <= MLA_LOGIT_BOUND,
        lambda: _mla_attn(qt_mla, k_mla, vt_mla, ts['mla_q_bounded'], ts['mla_k'], True),
        lambda: _mla_attn(qt_mla, k_mla, vt_mla, ts['mla_q'], ts['mla_k'], False))
    ot_swa = _swa_attn(qt_swa, k_swa, vt_swa, w['sink_rows'], ts['swa'])
    x1 = _merge(x, ot_mla, ot_swa, q_mem, kt_mem, v_mem, w, ts['merge'])
    return _ffn(x1, w, ts['ffn'])


def kernel(x_prompt, x_sample, mem_prompt, mem_sample, g_mix, g_mem, w_in, q_a_norm, w_q_b, kv_a_norm, w_kv_b,
           g_q_mla, g_k_mla, g_q_swa, g_k_swa, swa_sink, w_mem_kv, g_q_mem, g_k_mem, w_o_mla, w_o_swa, w_o_mem,
           w_out, g_ffn, w_up, conv_w, conv_b, w_down):
    weights = (g_mix, g_mem, w_in, q_a_norm, w_q_b, kv_a_norm, w_kv_b, g_q_mla, g_k_mla, g_q_swa, g_k_swa,
               swa_sink, w_mem_kv, g_q_mem, g_k_mem, w_o_mla, w_o_swa, w_o_mem, w_out, g_ffn, w_up, conv_w,
               conv_b, w_down)
    depth = g_mix.shape[0]
    y_prompt, y_sample = x_prompt, x_sample
    tabs = _rope_tables(max(x_prompt.shape[1], x_sample.shape[1]))
    for layer in range(depth):
        w = _prep_weights(*(p[layer] for p in weights))
        y_prompt = _layer(y_prompt, mem_prompt, w, tabs)
        y_sample = _layer(y_sample, mem_sample, w, tabs)
    return (y_prompt, y_sample)
```

```python
import functools
import math

import jax
import jax.numpy as jnp
from jax import lax
from jax.experimental import pallas as pl
from jax.experimental.pallas import tpu as pltpu

D_MODEL = 1024
N_MEM = 256
EPS = 1e-6
ROPE_THETA = 10000.0
MLA_HEADS = 8
MLA_NOPE = 64
MLA_ROPE = 32
MLA_V = 64
MLA_QK = MLA_NOPE + MLA_ROPE
Q_LORA = 384
KV_LORA = 256
SWA_HEADS = 8
SWA_KV_HEADS = 2
SWA_GROUP = SWA_HEADS // SWA_KV_HEADS
SWA_HD = 64
WINDOW = 128
BLOCK = 128
MEM_HEADS = 4
MEM_HD = 128
N_BRANCH = 3
D_FF = 2816
SPLITS = (Q_LORA, KV_LORA, MLA_ROPE, SWA_HEADS * SWA_HD, SWA_KV_HEADS * SWA_HD,
          SWA_KV_HEADS * SWA_HD, MEM_HEADS * MEM_HD, N_BRANCH * D_MODEL)

LANE = 128
MLA_HP = 128
MLA_VROWS = 80
LOG2E = math.log2(math.e)
NEG_BIG = -1e30
MLA_LOGIT_BOUND = 80.0
VMEM_LIMIT = 56 * 1024 * 1024

BF16 = jnp.bfloat16
F32 = jnp.float32


def _dot(a, b):
    return jnp.dot(a, b, preferred_element_type=F32)


def _dot_nt(a, b):
    return lax.dot_general(a, b, (((1,), (1,)), ((), ())), preferred_element_type=F32)


def _dot_tn(a, b):
    return lax.dot_general(a, b, (((0,), (0,)), ((), ())), preferred_element_type=F32)


def _const_spec(shape):
    nd = len(shape)
    return pl.BlockSpec(shape, lambda *_: (0,) * nd, pipeline_mode=pl.Buffered(1))


def _params(semantics):
    return pltpu.CompilerParams(dimension_semantics=semantics, vmem_limit_bytes=VMEM_LIMIT)


def _proj_kernel(x_ref, gmix_ref, wa_ref, wbt_ref, qan_ref, wqt_ref, kvan_ref, wk_ref, wvt_ref,
                 gq_mla_ref, gk_mla_ref, gq_swa_ref, gk_swa_ref, gq_mem_ref,
                 ct16_ref, st16_ref, ck_ref, sak_ref, sbk_ref,
                 ct32_ref, st32_ref, cs_ref, sas_ref, sbs_ref,
                 qt_mla_ref, k_mla_ref, vt_mla_ref, qt_swa_ref, k_swa_ref, vt_swa_ref, q_mem_ref):
    x = x_ref[...]
    tm = x.shape[0]
    h = x * lax.rsqrt(jnp.mean(x * x, axis=-1, keepdims=True) + EPS) * gmix_ref[...]
    hb = h.astype(BF16)
    z = _dot(hb, wa_ref[...])
    c_q = z[:, 0:384]
    c_kv = z[:, 384:640]
    k_s = z[:, 640:768]
    q_m = z[:, 768:1280]
    k_rope = z[:, 1280:1408]

    bt = _dot_nt(wbt_ref[...], hb)
    c_qn = (c_q * lax.rsqrt(jnp.mean(c_q * c_q, axis=-1, keepdims=True) + EPS) * qan_ref[...]).astype(BF16)
    c_kvn = (c_kv * lax.rsqrt(jnp.mean(c_kv * c_kv, axis=-1, keepdims=True) + EPS) * kvan_ref[...]).astype(BF16)
    qt = _dot_nt(wqt_ref[...], c_qn)
    kpre = _dot(c_kvn, wk_ref[...])
    vt = _dot_nt(wvt_ref[...], c_kvn)

    ct = ct16_ref[...]
    st = st16_ref[...]
    gq = gq_mla_ref[...]
    for hd in range(MLA_HEADS):
        blk = qt[hd * MLA_HP:(hd + 1) * MLA_HP]
        ss = jnp.sum(blk * blk, axis=0, keepdims=True)
        y = blk * lax.rsqrt(ss * (1.0 / MLA_QK) + EPS) * gq
        x1 = y[64:80]
        x2 = y[80:96]
        out = jnp.concatenate([y[0:64], x1 * ct - x2 * st, x2 * ct + x1 * st, y[96:128]], axis=0)
        qt_mla_ref[hd * MLA_HP:(hd + 1) * MLA_HP, :] = out.astype(BF16)

    gk = gk_mla_ref[...]
    ss_rope = jnp.sum(k_rope * k_rope, axis=-1, keepdims=True)
    yr = k_rope * gk
    k_roped = yr * ck_ref[...] + pltpu.roll(yr, 112, 1) * sak_ref[...] + pltpu.roll(yr, 16, 1) * sbk_ref[...]
    for hd in range(MLA_HEADS):
        blk = kpre[:, hd * MLA_HP:(hd + 1) * MLA_HP]
        ss = jnp.sum(blk * blk, axis=-1, keepdims=True) + ss_rope
        out = (blk * gk + k_roped) * lax.rsqrt(ss * (1.0 / MLA_QK) + EPS)
        k_mla_ref[hd] = out.astype(BF16)
    ones = jnp.ones((MLA_VROWS - MLA_V, tm), BF16)
    for hd in range(MLA_HEADS):
        vt_mla_ref[hd * MLA_VROWS:hd * MLA_VROWS + MLA_V, :] = vt[hd * MLA_V:(hd + 1) * MLA_V].astype(BF16)
        vt_mla_ref[hd * MLA_VROWS + MLA_V:(hd + 1) * MLA_VROWS, :] = ones

    c32 = ct32_ref[...]
    s32 = st32_ref[...]
    gqs = gq_swa_ref[...]
    for hd in range(SWA_HEADS):
        blk = bt[hd * SWA_HD:(hd + 1) * SWA_HD]
        ss = jnp.sum(blk * blk, axis=0, keepdims=True)
        y = blk * lax.rsqrt(ss * (1.0 / SWA_HD) + EPS) * gqs
        x1 = y[0:32]
        x2 = y[32:64]
        out = jnp.concatenate([x1 * c32 - x2 * s32, x2 * c32 + x1 * s32], axis=0)
        qt_swa_ref[hd * SWA_HD:(hd + 1) * SWA_HD, :] = out.astype(BF16)
    vt_swa_ref[...] = bt[SWA_HEADS * SWA_HD:].astype(BF16)
    lane = lax.broadcasted_iota(jnp.int32, (tm, LANE), 1)
    lo = lane < SWA_HD
    sq = k_s * k_s
    ss_lo = jnp.sum(jnp.where(lo, sq, 0.0), axis=-1, keepdims=True)
    ss_hi = jnp.sum(jnp.where(lo, 0.0, sq), axis=-1, keepdims=True)
    rstd = jnp.where(lo, lax.rsqrt(ss_lo * (1.0 / SWA_HD) + EPS), lax.rsqrt(ss_hi * (1.0 / SWA_HD) + EPS))
    y = k_s * rstd * gk_swa_ref[...]
    out = y * cs_ref[...] + pltpu.roll(y, 96, 1) * sas_ref[...] + pltpu.roll(y, 32, 1) * sbs_ref[...]
    k_swa_ref[...] = out.astype(BF16)

    gqm = gq_mem_ref[...]
    for hd in range(MEM_HEADS):
        blk = q_m[:, hd * MEM_HD:(hd + 1) * MEM_HD]
        ss = jnp.sum(blk * blk, axis=-1, keepdims=True)
        y = blk * lax.rsqrt(ss * (1.0 / MEM_HD) + EPS) * gqm
        q_mem_ref[:, hd * MEM_HD:(hd + 1) * MEM_HD] = y.astype(BF16)


def _proj(x, w, tabs, tm):
    b, s, d = x.shape
    grid = (s // tm, b)
    tok = lambda width: pl.BlockSpec((None, tm, width), lambda j, i: (i, j, 0))
    tok_t = lambda rows: pl.BlockSpec((None, rows, tm), lambda j, i: (i, 0, j))
    tab = lambda width: pl.BlockSpec((tm, width), lambda j, i: (j, 0))
    tab_t = lambda rows: pl.BlockSpec((rows, tm), lambda j, i: (0, j))
    consts = [w['g_mix'], w['wa'], w['wbt'], w['q_a_norm'], w['wqt'], w['kv_a_norm'], w['wk'], w['wvt'],
              w['gq_mla'], w['gk_mla'], w['gq_swa'], w['gk_swa'], w['gq_mem']]
    in_specs = [tok(d)] + [_const_spec(c.shape) for c in consts] + [
        tab_t(16), tab_t(16), tab(LANE), tab(LANE), tab(LANE),
        tab_t(32), tab_t(32), tab(LANE), tab(LANE), tab(LANE)]
    out_shape = (
        jax.ShapeDtypeStruct((b, MLA_HEADS * MLA_HP, s), BF16),
        jax.ShapeDtypeStruct((b, MLA_HEADS, s, MLA_HP), BF16),
        jax.ShapeDtypeStruct((b, MLA_HEADS * MLA_VROWS, s), BF16),
        jax.ShapeDtypeStruct((b, SWA_HEADS * SWA_HD, s), BF16),
        jax.ShapeDtypeStruct((b, s, SWA_KV_HEADS * SWA_HD), BF16),
        jax.ShapeDtypeStruct((b, SWA_KV_HEADS * SWA_HD, s), BF16),
        jax.ShapeDtypeStruct((b, s, MEM_HEADS * MEM_HD), BF16),
    )
    k_heads = pl.BlockSpec((None, MLA_HEADS, tm, MLA_HP), lambda j, i: (i, 0, j, 0))
    out_specs = (tok_t(MLA_HEADS * MLA_HP), k_heads, tok_t(MLA_HEADS * MLA_VROWS),
                 tok_t(SWA_HEADS * SWA_HD), tok(SWA_KV_HEADS * SWA_HD), tok_t(SWA_KV_HEADS * SWA_HD),
                 tok(MEM_HEADS * MEM_HD))
    return pl.pallas_call(
        _proj_kernel, grid=grid, in_specs=in_specs, out_specs=out_specs, out_shape=out_shape,
        compiler_params=_params(("parallel", "parallel")), name="proj",
    )(x, *consts, tabs['ct16'], tabs['st16'], tabs['ck'], tabs['sak'], tabs['sbk'],
      tabs['ct32'], tabs['st32'], tabs['cs'], tabs['sas'], tabs['sbs'])


def _mem_kv_kernel(mem_ref, gmem_ref, wkt_ref, wv_ref, gk_ref, kt_ref, v_ref):
    m = mem_ref[...]
    mn = (m * lax.rsqrt(jnp.mean(m * m, axis=-1, keepdims=True) + EPS) * gmem_ref[...]).astype(BF16)
    kt = _dot_nt(wkt_ref[...], mn)
    gk = gk_ref[...]
    for hd in range(MEM_HEADS):
        blk = kt[hd * MEM_HD:(hd + 1) * MEM_HD]
        ss = jnp.sum(blk * blk, axis=0, keepdims=True)
        kt_ref[hd * MEM_HD:(hd + 1) * MEM_HD, :] = (blk * lax.rsqrt(ss * (1.0 / MEM_HD) + EPS) * gk).astype(BF16)
    v_ref[...] = _dot(mn, wv_ref[...]).astype(BF16)


def _mem_kv(mem, w):
    b, n, d = mem.shape
    hw = MEM_HEADS * MEM_HD
    consts = [w['g_mem'], w['wmkt'], w['wmv'], w['gk_mem']]
    return pl.pallas_call(
        _mem_kv_kernel, grid=(b,),
        in_specs=[pl.BlockSpec((None, n, d), lambda i: (i, 0, 0))] + [_const_spec(c.shape) for c in consts],
        out_specs=(pl.BlockSpec((None, hw, n), lambda i: (i, 0, 0)), pl.BlockSpec((None, n, hw), lambda i: (i, 0, 0))),
        out_shape=(jax.ShapeDtypeStruct((b, hw, n), BF16), jax.ShapeDtypeStruct((b, n, hw), BF16)),
        compiler_params=_params(("parallel",)), name="mem_kv",
    )(mem, *consts)


def _aligned(start, align):
    return start if isinstance(start, int) else pl.multiple_of(start, align)


def _mla_kernel(qt_ref, k_ref, vt_ref, ot_ref, *scratch, tile_fn, tq):
    def body(i, carry):
        cols = pl.ds(pl.multiple_of(i * tq, tq), tq)
        ot_ref[:, cols] = tile_fn(qt_ref[:, cols], k_ref, vt_ref, *scratch)
        return carry

    lax.fori_loop(0, qt_ref.shape[1] // tq, body, 0)


def _mla_online_tile(qt, k_ref, vt_ref, s0_ref, s1_ref, p0_ref, p1_ref, *, tk, sub):
    s = k_ref.shape[0]
    tq = qt.shape[1]
    n = s // tk
    s_refs = (s0_ref, s1_ref)
    p_refs = (p0_ref, p1_ref)

    def step(c, par, carry, logits=True, accum=True, exps=True):
        mc, m, alpha, acc = carry
        if exps:
            m_new = jnp.maximum(m, mc)
            alpha_new = jnp.exp2(m - m_new)
        if accum:
            acc = alpha * acc
        mc_next = None
        for j in range(tk // sub):
            blk = slice(j * sub, (j + 1) * sub)
            if logits:
                off = _aligned((c + 1) * tk + j * sub, sub)
                st = _dot(k_ref[pl.ds(off, sub), :], qt)
                s_refs[1 - par][blk, :] = st
                mj = jnp.max(st, axis=0, keepdims=True)
                mc_next = mj if mc_next is None else jnp.maximum(mc_next, mj)
            if accum:
                off = _aligned((c - 1) * tk + j * sub, sub)
                acc = acc + _dot(vt_ref[:, pl.ds(off, sub)], p_refs[1 - par][blk, :])
            if exps:
                p_refs[par][blk, :] = jnp.exp2((s_refs[par][blk, :] - m_new).astype(BF16))
        return (mc_next if logits else mc, m_new if exps else m, alpha_new if exps else alpha, acc)

    def pair(i, carry):
        return step(2 * i + 2, 0, step(2 * i + 1, 1, carry))

    init = jnp.full((1, tq), NEG_BIG, F32)
    carry = (init, init, jnp.zeros((1, tq), F32), jnp.zeros((MLA_VROWS, tq), F32))
    carry = step(-1, 1, carry, accum=False, exps=False)
    carry = step(0, 0, carry, accum=False)
    carry = lax.fori_loop(0, n // 2 - 1, pair, carry, unroll=True)
    carry = step(n - 1, 1, carry, logits=False)
    _, _, _, acc = step(n, 0, carry, logits=False, exps=False)
    return (acc[0:MLA_V] / acc[MLA_V:MLA_V + 1]).astype(BF16)


def _mla_bounded_tile(qt, k_ref, vt_ref, *, sub):
    s = k_ref.shape[0]
    acc = jnp.zeros((MLA_VROWS, qt.shape[1]), F32)
    for j in range(s // sub):
        blk = slice(j * sub, (j + 1) * sub)
        pt = jnp.exp2(_dot(k_ref[blk, :], qt)).astype(BF16)
        acc = acc + _dot(vt_ref[:, blk], pt)
    return (acc[0:MLA_V] / acc[MLA_V:MLA_V + 1]).astype(BF16)


def _mla_attn(qt, k, vt, tq, tk, bounded):
    b, _, s = qt.shape
    tq_step = min(s, 4 * tq)
    grid = (b, MLA_HEADS, s // tq_step)
    assert (s // tk) % 2 == 0 and s % tq_step == 0
    if bounded:
        tile_fn = functools.partial(_mla_bounded_tile, sub=min(s, 2048))
        scratch = []
    else:
        tile_fn = functools.partial(_mla_online_tile, tk=tk, sub=min(tk, 256))
        scratch = [pltpu.VMEM((tk, tq), F32), pltpu.VMEM((tk, tq), F32),
                   pltpu.VMEM((tk, tq), BF16), pltpu.VMEM((tk, tq), BF16)]
    return pl.pallas_call(
        functools.partial(_mla_kernel, tile_fn=tile_fn, tq=tq), grid=grid,
        in_specs=[pl.BlockSpec((None, MLA_HP, tq_step), lambda i, h, j: (i, h, j)),
                  pl.BlockSpec((None, None, s, MLA_HP), lambda i, h, j: (i, h, 0, 0)),
                  pl.BlockSpec((None, MLA_VROWS, s), lambda i, h, j: (i, h, 0))],
        out_specs=pl.BlockSpec((None, MLA_V, tq_step), lambda i, h, j: (i, h, j)),
        out_shape=jax.ShapeDtypeStruct((b, MLA_HEADS * MLA_V, s), BF16),
        scratch_shapes=scratch,
        compiler_params=_params(("parallel", "parallel", "arbitrary")),
        name="mla_attn_bounded" if bounded else "mla_attn",
    )(qt, k, vt)


SWA_SPAN = 3 * BLOCK


def _swa_bias():
    r = jnp.arange(SWA_SPAN)[:, None]
    c = jnp.arange(BLOCK)[None, :]
    return jnp.stack([jnp.where(jnp.abs(lead * BLOCK + c - r) <= WINDOW, 0.0, NEG_BIG) for lead in range(3)]).astype(F32)


def _swa_kernel(qt_ref, k_ref, vt_ref, sink_ref, bias_ref, ot_ref):
    s = k_ref.shape[0]
    t = qt_ref.shape[1]
    nb = s // BLOCK
    gw = SWA_GROUP * BLOCK
    zeros = jnp.zeros((SWA_HD, gw), BF16)
    ones = jnp.ones((16, SWA_SPAN), BF16)
    logits = []
    for blk in range(t // BLOCK):
        n = pl.program_id(1) * (t // BLOCK) + blk
        start = pl.multiple_of(jnp.clip((n - 1) * BLOCK, 0, s - SWA_SPAN), BLOCK)
        kwin = k_ref[pl.ds(start, SWA_SPAN), :]
        vwin = vt_ref[:, pl.ds(start, SWA_SPAN)]
        lead = jnp.where(n == 0, 0, jnp.where(n == nb - 1, 2, 1))
        bias = bias_ref[lead]
        bias = jnp.concatenate([bias] * SWA_GROUP, axis=1)
        for g in range(SWA_KV_HEADS):
            qg = jnp.concatenate(
                [qt_ref[(g * SWA_GROUP + j) * SWA_HD:(g * SWA_GROUP + j + 1) * SWA_HD, blk * BLOCK:(blk + 1) * BLOCK]
                 for j in range(SWA_GROUP)], axis=1)
            qpad = jnp.concatenate([qg, zeros] if g == 0 else [zeros, qg], axis=0)
            st = _dot(kwin, qpad) + bias
            vext = jnp.concatenate([vwin[g * SWA_HD:(g + 1) * SWA_HD], ones], axis=0)
            logits.append((blk, g, st, vext))
    weights = []
    for blk, g, st, vext in logits:
        sk = sink_ref[g:g + 1, :]
        m = jnp.maximum(jnp.max(st, axis=0, keepdims=True), sk)
        weights.append((blk, g, jnp.exp2((st - m).astype(BF16)), jnp.exp2(sk - m), vext))
    for blk, g, p, p_sink, vext in weights:
        acc = _dot(vext, p)
        ot = acc[0:SWA_HD] / (acc[SWA_HD:SWA_HD + 1] + p_sink)
        for j in range(SWA_GROUP):
            hd = g * SWA_GROUP + j
            ot_ref[hd * SWA_HD:(hd + 1) * SWA_HD, blk * BLOCK:(blk + 1) * BLOCK] = (
                ot[:, j * BLOCK:(j + 1) * BLOCK].astype(BF16))


def _swa_attn(qt, k, vt, sink_rows, t):
    b, hw, s = qt.shape
    kvw = SWA_KV_HEADS * SWA_HD
    bias = _swa_bias()
    return pl.pallas_call(
        _swa_kernel, grid=(b, s // t),
        in_specs=[pl.BlockSpec((None, hw, t), lambda i, j: (i, 0, j)),
                  pl.BlockSpec((None, s, kvw), lambda i, j: (i, 0, 0)),
                  pl.BlockSpec((None, kvw, s), lambda i, j: (i, 0, 0)),
                  _const_spec(sink_rows.shape), _const_spec(bias.shape)],
        out_specs=pl.BlockSpec((None, hw, t), lambda i, j: (i, 0, j)),
        out_shape=jax.ShapeDtypeStruct((b, hw, s), BF16),
        compiler_params=_params(("parallel", "arbitrary")), name="swa_attn",
    )(qt, k, vt, sink_rows, bias)


def _sigmoid(v):
    return 1.0 / (1.0 + jnp.exp(-v))


def _merge_kernel(x_ref, ot_mla_ref, ot_swa_ref, q_mem_ref, kt_mem_ref, v_mem_ref, gmix_ref, wg_ref,
                  wo_mla_ref, wo_swa_ref, wo_mem_ref, wout_ref, y_ref):
    x = x_ref[...]
    hb = (x * lax.rsqrt(jnp.mean(x * x, axis=-1, keepdims=True) + EPS) * gmix_ref[...]).astype(BF16)

    heads = [slice(hd * MEM_HD, (hd + 1) * MEM_HD) for hd in range(MEM_HEADS)]
    logits = [_dot(q_mem_ref[:, hs], kt_mem_ref[hs, :]) for hs in heads]
    probs = [jnp.exp2(sc - jnp.max(sc, axis=-1, keepdims=True)) for sc in logits]
    o_heads = [_dot(p.astype(BF16), v_mem_ref[:, hs]) / jnp.sum(p, axis=-1, keepdims=True)
               for p, hs in zip(probs, heads)]
    o_mem = jnp.concatenate(o_heads, axis=1).astype(BF16)

    merged = _sigmoid(_dot(hb, wg_ref[:, 0:D_MODEL])) * _dot_tn(ot_mla_ref[...], wo_mla_ref[...])
    merged += _sigmoid(_dot(hb, wg_ref[:, D_MODEL:2 * D_MODEL])) * _dot_tn(ot_swa_ref[...], wo_swa_ref[...])
    merged += _sigmoid(_dot(hb, wg_ref[:, 2 * D_MODEL:3 * D_MODEL])) * _dot(o_mem, wo_mem_ref[...])
    y_ref[...] = x + _dot(merged.astype(BF16), wout_ref[...])


def _merge(x, ot_mla, ot_swa, q_mem, kt_mem, v_mem, w, tm):
    b, s, d = x.shape
    n_mem = v_mem.shape[1]
    hw = MEM_HEADS * MEM_HD
    tok = lambda width: pl.BlockSpec((None, tm, width), lambda i, j: (i, j, 0))
    tok_t = lambda rows: pl.BlockSpec((None, rows, tm), lambda i, j: (i, 0, j))
    consts = [w['g_mix'], w['wg'], w['wo_mla'], w['wo_swa'], w['wo_mem'], w['wout']]
    return pl.pallas_call(
        _merge_kernel, grid=(b, s // tm),
        in_specs=[tok(d), tok_t(MLA_HEADS * MLA_V), tok_t(SWA_HEADS * SWA_HD), tok(hw),
                  pl.BlockSpec((None, hw, n_mem), lambda i, j: (i, 0, 0)),
                  pl.BlockSpec((None, n_mem, hw), lambda i, j: (i, 0, 0))] + [_const_spec(c.shape) for c in consts],
        out_specs=tok(d), out_shape=jax.ShapeDtypeStruct((b, s, d), F32),
        compiler_params=_params(("parallel", "parallel")), name="merge",
    )(x, ot_mla, ot_swa, q_mem, kt_mem, v_mem, *consts)


HALO = 8


def _ffn_kernel(x_ref, prev_ref, next_ref, gffn_ref, wup_ref, cw_ref, cb_ref, wdown_ref, y_ref):
    j = pl.program_id(1)
    nj = pl.num_programs(1)
    x = x_ref[...]
    tm = x.shape[0]
    g = gffn_ref[...]

    def norm(v):
        return v * lax.rsqrt(jnp.mean(v * v, axis=-1, keepdims=True) + EPS) * g

    hp = jnp.where(j > 0, norm(prev_ref[...]), 0.0)
    hn = jnp.where(j < nj - 1, norm(next_ref[...]), 0.0)
    hext = jnp.concatenate([hp, norm(x), hn], axis=0).astype(BF16)
    u = _dot(hext, wup_ref[...])
    cw = cw_ref[...]
    rows = tm + 2 * HALO
    conv = (pltpu.roll(u, 1, 0)[HALO:HALO + tm] * cw[0:1] + u[HALO:HALO + tm] * cw[1:2]
            + pltpu.roll(u, rows - 1, 0)[HALO:HALO + tm] * cw[2:3] + cb_ref[...])
    a = conv[:, :D_FF]
    val = conv[:, D_FF:]
    act = (a * _sigmoid(a) * val).astype(BF16)
    y_ref[...] = x + _dot(act, wdown_ref[...])


def _ffn(x, w, tm):
    b, s, d = x.shape
    nh = tm // HALO
    last = s // HALO - 1
    consts = [w['g_ffn'], w['wup'], w['conv_w'], w['conv_b'], w['wdown']]
    return pl.pallas_call(
        _ffn_kernel, grid=(b, s // tm),
        in_specs=[pl.BlockSpec((None, tm, d), lambda i, j: (i, j, 0)),
                  pl.BlockSpec((None, HALO, d), lambda i, j: (i, jnp.maximum(j * nh - 1, 0), 0)),
                  pl.BlockSpec((None, HALO, d), lambda i, j: (i, jnp.minimum((j + 1) * nh, last), 0))]
                 + [_const_spec(c.shape) for c in consts],
        out_specs=pl.BlockSpec((None, tm, d), lambda i, j: (i, j, 0)),
        out_shape=jax.ShapeDtypeStruct((b, s, d), F32),
        compiler_params=_params(("parallel", "parallel")), name="ffn",
    )(x, x, x, *consts)


def _prep_weights(g_mix, g_mem, w_in, q_a_norm, w_q_b, kv_a_norm, w_kv_b, g_q_mla, g_k_mla, g_q_swa, g_k_swa,
                  swa_sink, w_mem_kv, g_q_mem, g_k_mem, w_o_mla, w_o_swa, w_o_mem, w_out, g_ffn, w_up, conv_w,
                  conv_b, w_down):
    offs = [0]
    for sp in SPLITS:
        offs.append(offs[-1] + sp)
    w_cq, w_ckv, w_kr, w_qs, w_ks, w_vs, w_qm, w_gate = (w_in[:, offs[i]:offs[i + 1]] for i in range(8))
    row = lambda v: v.reshape(1, -1).astype(F32)
    col = lambda v: v.reshape(-1, 1).astype(F32)
    w = {}
    w['g_mix'] = row(g_mix)
    w['g_mem'] = row(g_mem)
    w['g_ffn'] = row(g_ffn)
    w['q_a_norm'] = row(q_a_norm)
    w['kv_a_norm'] = row(kv_a_norm)
    w_kr_placed = jnp.pad(w_kr, ((0, 0), (MLA_NOPE, MLA_HP - MLA_QK)))
    w['wa'] = jnp.concatenate([w_cq, w_ckv, w_ks, w_qm, w_kr_placed], axis=1).astype(BF16)
    w['wbt'] = jnp.concatenate([w_qs, w_vs], axis=1).T.astype(BF16)
    w['wg'] = w_gate.astype(BF16)
    wq = jnp.pad(w_q_b.reshape(Q_LORA, MLA_HEADS, MLA_QK), ((0, 0), (0, 0), (0, MLA_HP - MLA_QK)))
    w['wqt'] = wq.reshape(Q_LORA, MLA_HEADS * MLA_HP).T.astype(BF16)
    wkv = w_kv_b.reshape(KV_LORA, MLA_HEADS, MLA_NOPE + MLA_V)
    wk_nope = jnp.pad(wkv[:, :, :MLA_NOPE], ((0, 0), (0, 0), (0, MLA_HP - MLA_NOPE))).reshape(KV_LORA, -1)
    w['wk'] = wk_nope.astype(BF16)
    w['wvt'] = wkv[:, :, MLA_NOPE:].reshape(KV_LORA, MLA_HEADS * MLA_V).T.astype(BF16)
    sc_mla = MLA_QK ** -0.5 * LOG2E
    w['gq_mla'] = col(jnp.pad(g_q_mla * sc_mla, (0, MLA_HP - MLA_QK)))
    w['gk_mla'] = row(jnp.pad(g_k_mla, (0, MLA_HP - MLA_QK)))
    w['mla_logit_bound'] = 1.02 * MLA_QK * jnp.max(jnp.abs(w['gq_mla'])) * jnp.max(jnp.abs(w['gk_mla']))
    w['gq_swa'] = col(g_q_swa * (SWA_HD ** -0.5 * LOG2E))
    w['gk_swa'] = row(jnp.tile(g_k_swa, SWA_KV_HEADS))
    w['gq_mem'] = row(g_q_mem * (MEM_HD ** -0.5 * LOG2E))
    w['gk_mem'] = col(g_k_mem)
    w['sink_rows'] = jnp.repeat(swa_sink.astype(F32) * LOG2E, BLOCK).reshape(SWA_KV_HEADS, SWA_GROUP * BLOCK)
    hw = MEM_HEADS * MEM_HD
    w['wmkt'] = w_mem_kv[:, :hw].T.astype(BF16)
    w['wmv'] = w_mem_kv[:, hw:].astype(BF16)
    w['wo_mla'] = w_o_mla.astype(BF16)
    w['wo_swa'] = w_o_swa.astype(BF16)
    w['wo_mem'] = w_o_mem.astype(BF16)
    w['wout'] = w_out.astype(BF16)
    w['wup'] = w_up.astype(BF16)
    w['wdown'] = w_down.astype(BF16)
    w['conv_w'] = conv_w.astype(F32)
    w['conv_b'] = row(conv_b)
    return w


def _rope_tables(s):
    def cs(dim):
        inv = 1.0 / (ROPE_THETA ** (jnp.arange(0, dim, 2, dtype=F32) / dim))
        ang = jnp.arange(s, dtype=F32)[:, None] * inv[None, :]
        return jnp.cos(ang), jnp.sin(ang)

    c16, s16 = cs(MLA_ROPE)
    c32, s32 = cs(SWA_HD)
    z = lambda n: jnp.zeros((s, n), F32)
    t = {'ct16': c16.T, 'st16': s16.T, 'ct32': c32.T, 'st32': s32.T}
    t['ck'] = jnp.concatenate([jnp.ones((s, MLA_NOPE), F32), c16, c16, z(MLA_HP - MLA_QK)], axis=1)
    t['sak'] = jnp.concatenate([z(MLA_NOPE), -s16, z(16), z(MLA_HP - MLA_QK)], axis=1)
    t['sbk'] = jnp.concatenate([z(MLA_NOPE), z(16), s16, z(MLA_HP - MLA_QK)], axis=1)
    t['cs'] = jnp.concatenate([c32, c32] * SWA_KV_HEADS, axis=1)
    t['sas'] = jnp.concatenate([-s32, z(32)] * SWA_KV_HEADS, axis=1)
    t['sbs'] = jnp.concatenate([z(32), s32] * SWA_KV_HEADS, axis=1)
    return t


def _tiles(s):
    pick = lambda want: want if s % want == 0 else BLOCK
    mla_k = 1024 if s % 2048 == 0 else BLOCK
    return dict(proj=pick(1024), mla_q=pick(256), mla_q_bounded=pick(1024), mla_k=mla_k, swa=pick(1024),
                merge=pick(1024), ffn=pick(512))


def _layer(x, mem, w, tabs):
    b, s, d = x.shape
    assert d == D_MODEL and s % BLOCK == 0 and s >= 3 * BLOCK
    ts = _tiles(s)
    qt_mla, k_mla, vt_mla, qt_swa, k_swa, vt_swa, q_mem = _proj(x, w, tabs, ts['proj'])
    kt_mem, v_mem = _mem_kv(mem, w)
    ot_mla = lax.cond(
        w['mla_logit_bound'] <= MLA_LOGIT_BOUND,
        lambda: _mla_attn(qt_mla, k_mla, vt_mla, ts['mla_q_bounded'], ts['mla_k'], True),
        lambda: _mla_attn(qt_mla, k_mla, vt_mla, ts['mla_q'], ts['mla_k'], False))
    ot_swa = _swa_attn(qt_swa, k_swa, vt_swa, w['sink_rows'], ts['swa'])
    x1 = _merge(x, ot_mla, ot_swa, q_mem, kt_mem, v_mem, w, ts['merge'])
    return _ffn(x1, w, ts['ffn'])


def kernel(x_prompt, x_sample, mem_prompt, mem_sample, g_mix, g_mem, w_in, q_a_norm, w_q_b, kv_a_norm, w_kv_b,
           g_q_mla, g_k_mla, g_q_swa, g_k_swa, swa_sink, w_mem_kv, g_q_mem, g_k_mem, w_o_mla, w_o_swa, w_o_mem,
           w_out, g_ffn, w_up, conv_w, conv_b, w_down):
    weights = (g_mix, g_mem, w_in, q_a_norm, w_q_b, kv_a_norm, w_kv_b, g_q_mla, g_k_mla, g_q_swa, g_k_swa,
               swa_sink, w_mem_kv, g_q_mem, g_k_mem, w_o_mla, w_o_swa, w_o_mem, w_out, g_ffn, w_up, conv_w,
               conv_b, w_down)
    depth = g_mix.shape[0]
    y_prompt, y_sample = x_prompt, x_sample
    tabs = _rope_tables(max(x_prompt.shape[1], x_sample.shape[1]))
    for layer in range(depth):
        w = _prep_weights(*(p[layer] for p in weights))
        y_prompt = _layer(y_prompt, mem_prompt, w, tabs)
        y_sample = _layer(y_sample, mem_sample, w, tabs)
    return (y_prompt, y_sample)
```

```python
import functools
import math

import jax
import jax.numpy as jnp
from jax import lax
from jax.experimental import pallas as pl
from jax.experimental.pallas import tpu as pltpu

D_MODEL = 1024
N_MEM = 256
EPS = 1e-6
ROPE_THETA = 10000.0
MLA_HEADS = 8
MLA_NOPE = 64
MLA_ROPE = 32
MLA_V = 64
MLA_QK = MLA_NOPE + MLA_ROPE
Q_LORA = 384
KV_LORA = 256
SWA_HEADS = 8
SWA_KV_HEADS = 2
SWA_GROUP = SWA_HEADS // SWA_KV_HEADS
SWA_HD = 64
WINDOW = 128
BLOCK = 128
MEM_HEADS = 4
MEM_HD = 128
N_BRANCH = 3
D_FF = 2816
SPLITS = (Q_LORA, KV_LORA, MLA_ROPE, SWA_HEADS * SWA_HD, SWA_KV_HEADS * SWA_HD,
          SWA_KV_HEADS * SWA_HD, MEM_HEADS * MEM_HD, N_BRANCH * D_MODEL)

LANE = 128
MLA_HP = 128
MLA_VROWS = 80
LOG2E = math.log2(math.e)
NEG_BIG = -1e30
MLA_LOGIT_BOUND = 80.0
VMEM_LIMIT = 56 * 1024 * 1024

BF16 = jnp.bfloat16
F32 = jnp.float32


def _dot(a, b):
    return jnp.dot(a, b, preferred_element_type=F32)


def _dot_nt(a, b):
    return lax.dot_general(a, b, (((1,), (1,)), ((), ())), preferred_element_type=F32)


def _dot_tn(a, b):
    return lax.dot_general(a, b, (((0,), (0,)), ((), ())), preferred_element_type=F32)


def _const_spec(shape):
    nd = len(shape)
    return pl.BlockSpec(shape, lambda *_: (0,) * nd, pipeline_mode=pl.Buffered(1))


def _params(semantics):
    return pltpu.CompilerParams(dimension_semantics=semantics, vmem_limit_bytes=VMEM_LIMIT)


def _proj_kernel(x_ref, gmix_ref, wa_ref, wbt_ref, qan_ref, wqt_ref, kvan_ref, wk_ref, wvt_ref,
                 gq_mla_ref, gk_mla_ref, gq_swa_ref, gk_swa_ref, gq_mem_ref,
                 ct16_ref, st16_ref, ck_ref, sak_ref, sbk_ref,
                 ct32_ref, st32_ref, cs_ref, sas_ref, sbs_ref,
                 qt_mla_ref, k_mla_ref, vt_mla_ref, qt_swa_ref, k_swa_ref, vt_swa_ref, q_mem_ref):
    x = x_ref[...]
    tm = x.shape[0]
    h = x * lax.rsqrt(jnp.mean(x * x, axis=-1, keepdims=True) + EPS) * gmix_ref[...]
    hb = h.astype(BF16)
    z = _dot(hb, wa_ref[...])
    c_q = z[:, 0:384]
    c_kv = z[:, 384:640]
    k_s = z[:, 640:768]
    q_m = z[:, 768:1280]
    k_rope = z[:, 1280:1408]

    bt = _dot_nt(wbt_ref[...], hb)
    c_qn = (c_q * lax.rsqrt(jnp.mean(c_q * c_q, axis=-1, keepdims=True) + EPS) * qan_ref[...]).astype(BF16)
    c_kvn = (c_kv * lax.rsqrt(jnp.mean(c_kv * c_kv, axis=-1, keepdims=True) + EPS) * kvan_ref[...]).astype(BF16)
    qt = _dot_nt(wqt_ref[...], c_qn)
    kpre = _dot(c_kvn, wk_ref[...])
    vt = _dot_nt(wvt_ref[...], c_kvn)

    ct = ct16_ref[...]
    st = st16_ref[...]
    gq = gq_mla_ref[...]
    for hd in range(MLA_HEADS):
        blk = qt[hd * MLA_HP:(hd + 1) * MLA_HP]
        ss = jnp.sum(blk * blk, axis=0, keepdims=True)
        y = blk * lax.rsqrt(ss * (1.0 / MLA_QK) + EPS) * gq
        x1 = y[64:80]
        x2 = y[80:96]
        out = jnp.concatenate([y[0:64], x1 * ct - x2 * st, x2 * ct + x1 * st, y[96:128]], axis=0)
        qt_mla_ref[hd * MLA_HP:(hd + 1) * MLA_HP, :] = out.astype(BF16)

    gk = gk_mla_ref[...]
    ss_rope = jnp.sum(k_rope * k_rope, axis=-1, keepdims=True)
    yr = k_rope * gk
    k_roped = yr * ck_ref[...] + pltpu.roll(yr, 112, 1) * sak_ref[...] + pltpu.roll(yr, 16, 1) * sbk_ref[...]
    for hd in range(MLA_HEADS):
        blk = kpre[:, hd * MLA_HP:(hd + 1) * MLA_HP]
        ss = jnp.sum(blk * blk, axis=-1, keepdims=True) + ss_rope
        out = (blk * gk + k_roped) * lax.rsqrt(ss * (1.0 / MLA_QK) + EPS)
        k_mla_ref[hd] = out.astype(BF16)
    ones = jnp.ones((MLA_VROWS - MLA_V, tm), BF16)
    for hd in range(MLA_HEADS):
        vt_mla_ref[hd * MLA_VROWS:hd * MLA_VROWS + MLA_V, :] = vt[hd * MLA_V:(hd + 1) * MLA_V].astype(BF16)
        vt_mla_ref[hd * MLA_VROWS + MLA_V:(hd + 1) * MLA_VROWS, :] = ones

    c32 = ct32_ref[...]
    s32 = st32_ref[...]
    gqs = gq_swa_ref[...]
    for hd in range(SWA_HEADS):
        blk = bt[hd * SWA_HD:(hd + 1) * SWA_HD]
        ss = jnp.sum(blk * blk, axis=0, keepdims=True)
        y = blk * lax.rsqrt(ss * (1.0 / SWA_HD) + EPS) * gqs
        x1 = y[0:32]
        x2 = y[32:64]
        out = jnp.concatenate([x1 * c32 - x2 * s32, x2 * c32 + x1 * s32], axis=0)
        qt_swa_ref[hd * SWA_HD:(hd + 1) * SWA_HD, :] = out.astype(BF16)
    vt_swa_ref[...] = bt[SWA_HEADS * SWA_HD:].astype(BF16)
    lane = lax.broadcasted_iota(jnp.int32, (tm, LANE), 1)
    lo = lane < SWA_HD
    sq = k_s * k_s
    ss_lo = jnp.sum(jnp.where(lo, sq, 0.0), axis=-1, keepdims=True)
    ss_hi = jnp.sum(jnp.where(lo, 0.0, sq), axis=-1, keepdims=True)
    rstd = jnp.where(lo, lax.rsqrt(ss_lo * (1.0 / SWA_HD) + EPS), lax.rsqrt(ss_hi * (1.0 / SWA_HD) + EPS))
    y = k_s * rstd * gk_swa_ref[...]
    out = y * cs_ref[...] + pltpu.roll(y, 96, 1) * sas_ref[...] + pltpu.roll(y, 32, 1) * sbs_ref[...]
    k_swa_ref[...] = out.astype(BF16)

    gqm = gq_mem_ref[...]
    for hd in range(MEM_HEADS):
        blk = q_m[:, hd * MEM_HD:(hd + 1) * MEM_HD]
        ss = jnp.sum(blk * blk, axis=-1, keepdims=True)
        y = blk * lax.rsqrt(ss * (1.0 / MEM_HD) + EPS) * gqm
        q_mem_ref[:, hd * MEM_HD:(hd + 1) * MEM_HD] = y.astype(BF16)


def _proj(x, w, tabs, tm):
    b, s, d = x.shape
    grid = (s // tm, b)
    tok = lambda width: pl.BlockSpec((None, tm, width), lambda j, i: (i, j, 0))
    tok_t = lambda rows: pl.BlockSpec((None, rows, tm), lambda j, i: (i, 0, j))
    tab = lambda width: pl.BlockSpec((tm, width), lambda j, i: (j, 0))
    tab_t = lambda rows: pl.BlockSpec((rows, tm), lambda j, i: (0, j))
    consts = [w['g_mix'], w['wa'], w['wbt'], w['q_a_norm'], w['wqt'], w['kv_a_norm'], w['wk'], w['wvt'],
              w['gq_mla'], w['gk_mla'], w['gq_swa'], w['gk_swa'], w['gq_mem']]
    in_specs = [tok(d)] + [_const_spec(c.shape) for c in consts] + [
        tab_t(16), tab_t(16), tab(LANE), tab(LANE), tab(LANE),
        tab_t(32), tab_t(32), tab(LANE), tab(LANE), tab(LANE)]
    out_shape = (
        jax.ShapeDtypeStruct((b, MLA_HEADS * MLA_HP, s), BF16),
        jax.ShapeDtypeStruct((b, MLA_HEADS, s, MLA_HP), BF16),
        jax.ShapeDtypeStruct((b, MLA_HEADS * MLA_VROWS, s), BF16),
        jax.ShapeDtypeStruct((b, SWA_HEADS * SWA_HD, s), BF16),
        jax.ShapeDtypeStruct((b, s, SWA_KV_HEADS * SWA_HD), BF16),
        jax.ShapeDtypeStruct((b, SWA_KV_HEADS * SWA_HD, s), BF16),
        jax.ShapeDtypeStruct((b, s, MEM_HEADS * MEM_HD), BF16),
    )
    k_heads = pl.BlockSpec((None, MLA_HEADS, tm, MLA_HP), lambda j, i: (i, 0, j, 0))
    out_specs = (tok_t(MLA_HEADS * MLA_HP), k_heads, tok_t(MLA_HEADS * MLA_VROWS),
                 tok_t(SWA_HEADS * SWA_HD), tok(SWA_KV_HEADS * SWA_HD), tok_t(SWA_KV_HEADS * SWA_HD),
                 tok(MEM_HEADS * MEM_HD))
    return pl.pallas_call(
        _proj_kernel, grid=grid, in_specs=in_specs, out_specs=out_specs, out_shape=out_shape,
        compiler_params=_params(("parallel", "parallel")), name="proj",
    )(x, *consts, tabs['ct16'], tabs['st16'], tabs['ck'], tabs['sak'], tabs['sbk'],
      tabs['ct32'], tabs['st32'], tabs['cs'], tabs['sas'], tabs['sbs'])


def _mem_kv_kernel(mem_ref, gmem_ref, wkt_ref, wv_ref, gk_ref, kt_ref, v_ref):
    m = mem_ref[...]
    mn = (m * lax.rsqrt(jnp.mean(m * m, axis=-1, keepdims=True) + EPS) * gmem_ref[...]).astype(BF16)
    kt = _dot_nt(wkt_ref[...], mn)
    gk = gk_ref[...]
    for hd in range(MEM_HEADS):
        blk = kt[hd * MEM_HD:(hd + 1) * MEM_HD]
        ss = jnp.sum(blk * blk, axis=0, keepdims=True)
        kt_ref[hd * MEM_HD:(hd + 1) * MEM_HD, :] = (blk * lax.rsqrt(ss * (1.0 / MEM_HD) + EPS) * gk).astype(BF16)
    v_ref[...] = _dot(mn, wv_ref[...]).astype(BF16)


def _mem_kv(mem, w):
    b, n, d = mem.shape
    hw = MEM_HEADS * MEM_HD
    consts = [w['g_mem'], w['wmkt'], w['wmv'], w['gk_mem']]
    return pl.pallas_call(
        _mem_kv_kernel, grid=(b,),
        in_specs=[pl.BlockSpec((None, n, d), lambda i: (i, 0, 0))] + [_const_spec(c.shape) for c in consts],
        out_specs=(pl.BlockSpec((None, hw, n), lambda i: (i, 0, 0)), pl.BlockSpec((None, n, hw), lambda i: (i, 0, 0))),
        out_shape=(jax.ShapeDtypeStruct((b, hw, n), BF16), jax.ShapeDtypeStruct((b, n, hw), BF16)),
        compiler_params=_params(("parallel",)), name="mem_kv",
    )(mem, *consts)


def _aligned(start, align):
    return start if isinstance(start, int) else pl.multiple_of(start, align)


def _mla_kernel(qt_ref, k_ref, vt_ref, ot_ref, *scratch, tile_fn, tq):
    def body(i, carry):
        cols = pl.ds(pl.multiple_of(i * tq, tq), tq)
        ot_ref[:, cols] = tile_fn(qt_ref[:, cols], k_ref, vt_ref, *scratch)
        return carry

    lax.fori_loop(0, qt_ref.shape[1] // tq, body, 0)


def _mla_online_tile(qt, k_ref, vt_ref, s0_ref, s1_ref, p0_ref, p1_ref, *, tk, sub):
    s = k_ref.shape[0]
    tq = qt.shape[1]
    n = s // tk
    s_refs = (s0_ref, s1_ref)
    p_refs = (p0_ref, p1_ref)

    def step(c, par, carry, logits=True, accum=True, exps=True):
        mc, m, alpha, acc = carry
        if exps:
            m_new = jnp.maximum(m, mc)
            alpha_new = jnp.exp2(m - m_new)
        if accum:
            acc = alpha * acc
        mc_next = None
        for j in range(tk // sub):
            blk = slice(j * sub, (j + 1) * sub)
            if logits:
                off = _aligned((c + 1) * tk + j * sub, sub)
                st = _dot(k_ref[pl.ds(off, sub), :], qt)
                s_refs[1 - par][blk, :] = st
                mj = jnp.max(st, axis=0, keepdims=True)
                mc_next = mj if mc_next is None else jnp.maximum(mc_next, mj)
            if accum:
                off = _aligned((c - 1) * tk + j * sub, sub)
                acc = acc + _dot(vt_ref[:, pl.ds(off, sub)], p_refs[1 - par][blk, :])
            if exps:
                p_refs[par][blk, :] = jnp.exp2((s_refs[par][blk, :] - m_new).astype(BF16))
        return (mc_next if logits else mc, m_new if exps else m, alpha_new if exps else alpha, acc)

    def pair(i, carry):
        return step(2 * i + 2, 0, step(2 * i + 1, 1, carry))

    init = jnp.full((1, tq), NEG_BIG, F32)
    carry = (init, init, jnp.zeros((1, tq), F32), jnp.zeros((MLA_VROWS, tq), F32))
    carry = step(-1, 1, carry, accum=False, exps=False)
    carry = step(0, 0, carry, accum=False)
    carry = lax.fori_loop(0, n // 2 - 1, pair, carry, unroll=True)
    carry = step(n - 1, 1, carry, logits=False)
    _, _, _, acc = step(n, 0, carry, logits=False, exps=False)
    return (acc[0:MLA_V] / acc[MLA_V:MLA_V + 1]).astype(BF16)


def _mla_bounded_tile(qt, k_ref, vt_ref, *, sub):
    s = k_ref.shape[0]
    acc = jnp.zeros((MLA_VROWS, qt.shape[1]), F32)
    for j in range(s // sub):
        blk = slice(j * sub, (j + 1) * sub)
        pt = jnp.exp2(_dot(k_ref[blk, :], qt)).astype(BF16)
        acc = acc + _dot(vt_ref[:, blk], pt)
    return (acc[0:MLA_V] / acc[MLA_V:MLA_V + 1]).astype(BF16)


def _mla_attn(qt, k, vt, tq, tk, bounded):
    b, _, s = qt.shape
    tq_step = min(s, 4 * tq)
    grid = (b, MLA_HEADS, s // tq_step)
    assert (s // tk) % 2 == 0 and s % tq_step == 0
    if bounded:
        tile_fn = functools.partial(_mla_bounded_tile, sub=min(s, 2048))
        scratch = []
    else:
        tile_fn = functools.partial(_mla_online_tile, tk=tk, sub=min(tk, 256))
        scratch = [pltpu.VMEM((tk, tq), F32), pltpu.VMEM((tk, tq), F32),
                   pltpu.VMEM((tk, tq), BF16), pltpu.VMEM((tk, tq), BF16)]
    return pl.pallas_call(
        functools.partial(_mla_kernel, tile_fn=tile_fn, tq=tq), grid=grid,
        in_specs=[pl.BlockSpec((None, MLA_HP, tq_step), lambda i, h, j: (i, h, j)),
                  pl.BlockSpec((None, None, s, MLA_HP), lambda i, h, j: (i, h, 0, 0)),
                  pl.BlockSpec((None, MLA_VROWS, s), lambda i, h, j: (i, h, 0))],
        out_specs=pl.BlockSpec((None, MLA_V, tq_step), lambda i, h, j: (i, h, j)),
        out_shape=jax.ShapeDtypeStruct((b, MLA_HEADS * MLA_V, s), BF16),
        scratch_shapes=scratch,
        compiler_params=_params(("parallel", "parallel", "arbitrary")),
        name="mla_attn_bounded" if bounded else "mla_attn",
    )(qt, k, vt)


SWA_SPAN = 3 * BLOCK


def _swa_bias():
    r = jnp.arange(SWA_SPAN)[:, None]
    c = jnp.arange(BLOCK)[None, :]
    return jnp.stack([jnp.where(jnp.abs(lead * BLOCK + c - r) <= WINDOW, 0.0, NEG_BIG) for lead in range(3)]).astype(F32)


def _swa_kernel(qt_ref, k_ref, vt_ref, sink_ref, bias_ref, ot_ref):
    s = k_ref.shape[0]
    t = qt_ref.shape[1]
    nb = s // BLOCK
    gw = SWA_GROUP * BLOCK
    zeros = jnp.zeros((SWA_HD, gw), BF16)
    ones = jnp.ones((16, SWA_SPAN), BF16)
    logits = []
    for blk in range(t // BLOCK):
        n = pl.program_id(1) * (t // BLOCK) + blk
        start = pl.multiple_of(jnp.clip((n - 1) * BLOCK, 0, s - SWA_SPAN), BLOCK)
        kwin = k_ref[pl.ds(start, SWA_SPAN), :]
        vwin = vt_ref[:, pl.ds(start, SWA_SPAN)]
        lead = jnp.where(n == 0, 0, jnp.where(n == nb - 1, 2, 1))
        bias = bias_ref[lead]
        bias = jnp.concatenate([bias] * SWA_GROUP, axis=1)
        for g in range(SWA_KV_HEADS):
            qg = jnp.concatenate(
                [qt_ref[(g * SWA_GROUP + j) * SWA_HD:(g * SWA_GROUP + j + 1) * SWA_HD, blk * BLOCK:(blk + 1) * BLOCK]
                 for j in range(SWA_GROUP)], axis=1)
            qpad = jnp.concatenate([qg, zeros] if g == 0 else [zeros, qg], axis=0)
            st = _dot(kwin, qpad) + bias
            vext = jnp.concatenate([vwin[g * SWA_HD:(g + 1) * SWA_HD], ones], axis=0)
            logits.append((blk, g, st, vext))
    weights = []
    for blk, g, st, vext in logits:
        sk = sink_ref[g:g + 1, :]
        m = jnp.maximum(jnp.max(st, axis=0, keepdims=True), sk)
        weights.append((blk, g, jnp.exp2((st - m).astype(BF16)), jnp.exp2(sk - m), vext))
    for blk, g, p, p_sink, vext in weights:
        acc = _dot(vext, p)
        ot = acc[0:SWA_HD] / (acc[SWA_HD:SWA_HD + 1] + p_sink)
        for j in range(SWA_GROUP):
            hd = g * SWA_GROUP + j
            ot_ref[hd * SWA_HD:(hd + 1) * SWA_HD, blk * BLOCK:(blk + 1) * BLOCK] = (
                ot[:, j * BLOCK:(j + 1) * BLOCK].astype(BF16))


def _swa_attn(qt, k, vt, sink_rows, t):
    b, hw, s = qt.shape
    kvw = SWA_KV_HEADS * SWA_HD
    bias = _swa_bias()
    return pl.pallas_call(
        _swa_kernel, grid=(b, s // t),
        in_specs=[pl.BlockSpec((None, hw, t), lambda i, j: (i, 0, j)),
                  pl.BlockSpec((None, s, kvw), lambda i, j: (i, 0, 0)),
                  pl.BlockSpec((None, kvw, s), lambda i, j: (i, 0, 0)),
                  _const_spec(sink_rows.shape), _const_spec(bias.shape)],
        out_specs=pl.BlockSpec((None, hw, t), lambda i, j: (i, 0, j)),
        out_shape=jax.ShapeDtypeStruct((b, hw, s), BF16),
        compiler_params=_params(("parallel", "arbitrary")), name="swa_attn",
    )(qt, k, vt, sink_rows, bias)


def _sigmoid(v):
    return 1.0 / (1.0 + jnp.exp(-v))


def _merge_kernel(x_ref, ot_mla_ref, ot_swa_ref, q_mem_ref, kt_mem_ref, v_mem_ref, gmix_ref, wg_ref,
                  wo_mla_ref, wo_swa_ref, wo_mem_ref, wout_ref, y_ref):
    x = x_ref[...]
    hb = (x * lax.rsqrt(jnp.mean(x * x, axis=-1, keepdims=True) + EPS) * gmix_ref[...]).astype(BF16)

    heads = [slice(hd * MEM_HD, (hd + 1) * MEM_HD) for hd in range(MEM_HEADS)]
    logits = [_dot(q_mem_ref[:, hs], kt_mem_ref[hs, :]) for hs in heads]
    probs = [jnp.exp2(sc - jnp.max(sc, axis=-1, keepdims=True)) for sc in logits]
    o_heads = [_dot(p.astype(BF16), v_mem_ref[:, hs]) / jnp.sum(p, axis=-1, keepdims=True)
               for p, hs in zip(probs, heads)]
    o_mem = jnp.concatenate(o_heads, axis=1).astype(BF16)

    merged = _sigmoid(_dot(hb, wg_ref[:, 0:D_MODEL])) * _dot_tn(ot_mla_ref[...], wo_mla_ref[...])
    merged += _sigmoid(_dot(hb, wg_ref[:, D_MODEL:2 * D_MODEL])) * _dot_tn(ot_swa_ref[...], wo_swa_ref[...])
    merged += _sigmoid(_dot(hb, wg_ref[:, 2 * D_MODEL:3 * D_MODEL])) * _dot(o_mem, wo_mem_ref[...])
    y_ref[...] = x + _dot(merged.astype(BF16), wout_ref[...])


def _merge(x, ot_mla, ot_swa, q_mem, kt_mem, v_mem, w, tm):
    b, s, d = x.shape
    n_mem = v_mem.shape[1]
    hw = MEM_HEADS * MEM_HD
    tok = lambda width: pl.BlockSpec((None, tm, width), lambda i, j: (i, j, 0))
    tok_t = lambda rows: pl.BlockSpec((None, rows, tm), lambda i, j: (i, 0, j))
    consts = [w['g_mix'], w['wg'], w['wo_mla'], w['wo_swa'], w['wo_mem'], w['wout']]
    return pl.pallas_call(
        _merge_kernel, grid=(b, s // tm),
        in_specs=[tok(d), tok_t(MLA_HEADS * MLA_V), tok_t(SWA_HEADS * SWA_HD), tok(hw),
                  pl.BlockSpec((None, hw, n_mem), lambda i, j: (i, 0, 0)),
                  pl.BlockSpec((None, n_mem, hw), lambda i, j: (i, 0, 0))] + [_const_spec(c.shape) for c in consts],
        out_specs=tok(d), out_shape=jax.ShapeDtypeStruct((b, s, d), F32),
        compiler_params=_params(("parallel", "parallel")), name="merge",
    )(x, ot_mla, ot_swa, q_mem, kt_mem, v_mem, *consts)


HALO = 8


def _ffn_kernel(x_ref, prev_ref, next_ref, gffn_ref, wup_ref, cw_ref, cb_ref, wdown_ref, y_ref):
    j = pl.program_id(1)
    nj = pl.num_programs(1)
    x = x_ref[...]
    tm = x.shape[0]
    g = gffn_ref[...]

    def norm(v):
        return v * lax.rsqrt(jnp.mean(v * v, axis=-1, keepdims=True) + EPS) * g

    hp = jnp.where(j > 0, norm(prev_ref[...]), 0.0)
    hn = jnp.where(j < nj - 1, norm(next_ref[...]), 0.0)
    hext = jnp.concatenate([hp, norm(x), hn], axis=0).astype(BF16)
    u = _dot(hext, wup_ref[...])
    cw = cw_ref[...]
    rows = tm + 2 * HALO
    up, dn = pltpu.roll(u, 1, 0), pltpu.roll(u, rows - 1, 0)
    cb = cb_ref[...]
    half = tm // 2
    for r0 in (0, half):
        sl = slice(HALO + r0, HALO + r0 + half)
        conv = up[sl] * cw[0:1] + u[sl] * cw[1:2] + dn[sl] * cw[2:3] + cb
        a = conv[:, :D_FF]
        val = conv[:, D_FF:]
        act = (a * _sigmoid(a) * val).astype(BF16)
        y_ref[r0:r0 + half, :] = x[r0:r0 + half] + _dot(act, wdown_ref[...])


def _ffn(x, w, tm):
    b, s, d = x.shape
    nh = tm // HALO
    last = s // HALO - 1
    consts = [w['g_ffn'], w['wup'], w['conv_w'], w['conv_b'], w['wdown']]
    return pl.pallas_call(
        _ffn_kernel, grid=(b, s // tm),
        in_specs=[pl.BlockSpec((None, tm, d), lambda i, j: (i, j, 0)),
                  pl.BlockSpec((None, HALO, d), lambda i, j: (i, jnp.maximum(j * nh - 1, 0), 0)),
                  pl.BlockSpec((None, HALO, d), lambda i, j: (i, jnp.minimum((j + 1) * nh, last), 0))]
                 + [_const_spec(c.shape) for c in consts],
        out_specs=pl.BlockSpec((None, tm, d), lambda i, j: (i, j, 0)),
        out_shape=jax.ShapeDtypeStruct((b, s, d), F32),
        compiler_params=_params(("parallel", "parallel")), name="ffn",
    )(x, x, x, *consts)


def _prep_weights(g_mix, g_mem, w_in, q_a_norm, w_q_b, kv_a_norm, w_kv_b, g_q_mla, g_k_mla, g_q_swa, g_k_swa,
                  swa_sink, w_mem_kv, g_q_mem, g_k_mem, w_o_mla, w_o_swa, w_o_mem, w_out, g_ffn, w_up, conv_w,
                  conv_b, w_down):
    offs = [0]
    for sp in SPLITS:
        offs.append(offs[-1] + sp)
    w_cq, w_ckv, w_kr, w_qs, w_ks, w_vs, w_qm, w_gate = (w_in[:, offs[i]:offs[i + 1]] for i in range(8))
    row = lambda v: v.reshape(1, -1).astype(F32)
    col = lambda v: v.reshape(-1, 1).astype(F32)
    w = {}
    w['g_mix'] = row(g_mix)
    w['g_mem'] = row(g_mem)
    w['g_ffn'] = row(g_ffn)
    w['q_a_norm'] = row(q_a_norm)
    w['kv_a_norm'] = row(kv_a_norm)
    w_kr_placed = jnp.pad(w_kr, ((0, 0), (MLA_NOPE, MLA_HP - MLA_QK)))
    w['wa'] = jnp.concatenate([w_cq, w_ckv, w_ks, w_qm, w_kr_placed], axis=1).astype(BF16)
    w['wbt'] = jnp.concatenate([w_qs, w_vs], axis=1).T.astype(BF16)
    w['wg'] = w_gate.astype(BF16)
    wq = jnp.pad(w_q_b.reshape(Q_LORA, MLA_HEADS, MLA_QK), ((0, 0), (0, 0), (0, MLA_HP - MLA_QK)))
    w['wqt'] = wq.reshape(Q_LORA, MLA_HEADS * MLA_HP).T.astype(BF16)
    wkv = w_kv_b.reshape(KV_LORA, MLA_HEADS, MLA_NOPE + MLA_V)
    wk_nope = jnp.pad(wkv[:, :, :MLA_NOPE], ((0, 0), (0, 0), (0, MLA_HP - MLA_NOPE))).reshape(KV_LORA, -1)
    w['wk'] = wk_nope.astype(BF16)
    w['wvt'] = wkv[:, :, MLA_NOPE:].reshape(KV_LORA, MLA_HEADS * MLA_V).T.astype(BF16)
    sc_mla = MLA_QK ** -0.5 * LOG2E
    w['gq_mla'] = col(jnp.pad(g_q_mla * sc_mla, (0, MLA_HP - MLA_QK)))
    w['gk_mla'] = row(jnp.pad(g_k_mla, (0, MLA_HP - MLA_QK)))
    w['mla_logit_bound'] = 1.02 * MLA_QK * jnp.max(jnp.abs(w['gq_mla'])) * jnp.max(jnp.abs(w['gk_mla']))
    w['gq_swa'] = col(g_q_swa * (SWA_HD ** -0.5 * LOG2E))
    w['gk_swa'] = row(jnp.tile(g_k_swa, SWA_KV_HEADS))
    w['gq_mem'] = row(g_q_mem * (MEM_HD ** -0.5 * LOG2E))
    w['gk_mem'] = col(g_k_mem)
    w['sink_rows'] = jnp.repeat(swa_sink.astype(F32) * LOG2E, BLOCK).reshape(SWA_KV_HEADS, SWA_GROUP * BLOCK)
    hw = MEM_HEADS * MEM_HD
    w['wmkt'] = w_mem_kv[:, :hw].T.astype(BF16)
    w['wmv'] = w_mem_kv[:, hw:].astype(BF16)
    w['wo_mla'] = w_o_mla.astype(BF16)
    w['wo_swa'] = w_o_swa.astype(BF16)
    w['wo_mem'] = w_o_mem.astype(BF16)
    w['wout'] = w_out.astype(BF16)
    w['wup'] = w_up.astype(BF16)
    w['wdown'] = w_down.astype(BF16)
    w['conv_w'] = conv_w.astype(F32)
    w['conv_b'] = row(conv_b)
    return w


def _rope_tables(s):
    def cs(dim):
        inv = 1.0 / (ROPE_THETA ** (jnp.arange(0, dim, 2, dtype=F32) / dim))
        ang = jnp.arange(s, dtype=F32)[:, None] * inv[None, :]
        return jnp.cos(ang), jnp.sin(ang)

    c16, s16 = cs(MLA_ROPE)
    c32, s32 = cs(SWA_HD)
    z = lambda n: jnp.zeros((s, n), F32)
    t = {'ct16': c16.T, 'st16': s16.T, 'ct32': c32.T, 'st32': s32.T}
    t['ck'] = jnp.concatenate([jnp.ones((s, MLA_NOPE), F32), c16, c16, z(MLA_HP - MLA_QK)], axis=1)
    t['sak'] = jnp.concatenate([z(MLA_NOPE), -s16, z(16), z(MLA_HP - MLA_QK)], axis=1)
    t['sbk'] = jnp.concatenate([z(MLA_NOPE), z(16), s16, z(MLA_HP - MLA_QK)], axis=1)
    t['cs'] = jnp.concatenate([c32, c32] * SWA_KV_HEADS, axis=1)
    t['sas'] = jnp.concatenate([-s32, z(32)] * SWA_KV_HEADS, axis=1)
    t['sbs'] = jnp.concatenate([z(32), s32] * SWA_KV_HEADS, axis=1)
    return t


def _tiles(s):
    pick = lambda want: want if s % want == 0 else BLOCK
    mla_k = 1024 if s % 2048 == 0 else BLOCK
    return dict(proj=pick(1024), mla_q=pick(256), mla_q_bounded=pick(1024), mla_k=mla_k, swa=pick(1024),
                merge=pick(1024), ffn=pick(512))


def _layer(x, mem, w, tabs):
    b, s, d = x.shape
    assert d == D_MODEL and s % BLOCK == 0 and s >= 3 * BLOCK
    ts = _tiles(s)
    qt_mla, k_mla, vt_mla, qt_swa, k_swa, vt_swa, q_mem = _proj(x, w, tabs, ts['proj'])
    kt_mem, v_mem = _mem_kv(mem, w)
    ot_mla = lax.cond(
        w['mla_logit_bound'] <= MLA_LOGIT_BOUND,
        lambda: _mla_attn(qt_mla, k_mla, vt_mla, ts['mla_q_bounded'], ts['mla_k'], True),
        lambda: _mla_attn(qt_mla, k_mla, vt_mla, ts['mla_q'], ts['mla_k'], False))
    ot_swa = _swa_attn(qt_swa, k_swa, vt_swa, w['sink_rows'], ts['swa'])
    x1 = _merge(x, ot_mla, ot_swa, q_mem, kt_mem, v_mem, w, ts['merge'])
    return _ffn(x1, w, ts['ffn'])


def kernel(x_prompt, x_sample, mem_prompt, mem_sample, g_mix, g_mem, w_in, q_a_norm, w_q_b, kv_a_norm, w_kv_b,
           g_q_mla, g_k_mla, g_q_swa, g_k_swa, swa_sink, w_mem_kv, g_q_mem, g_k_mem, w_o_mla, w_o_swa, w_o_mem,
           w_out, g_ffn, w_up, conv_w, conv_b, w_down):
    weights = (g_mix, g_mem, w_in, q_a_norm, w_q_b, kv_a_norm, w_kv_b, g_q_mla, g_k_mla, g_q_swa, g_k_swa,
               swa_sink, w_mem_kv, g_q_mem, g_k_mem, w_o_mla, w_o_swa, w_o_mem, w_out, g_ffn, w_up, conv_w,
               conv_b, w_down)
    depth = g_mix.shape[0]
    y_prompt, y_sample = x_prompt, x_sample
    tabs = _rope_tables(max(x_prompt.shape[1], x_sample.shape[1]))
    for layer in range(depth):
        w = _prep_weights(*(p[layer] for p in weights))
        y_prompt = _layer(y_prompt, mem_prompt, w, tabs)
        y_sample = _layer(y_sample, mem_sample, w, tabs)
    return (y_prompt, y_sample)
```
